```python
import jax
import jax.numpy as jnp
from jax import lax
import numpy as np

D_MODEL = 1024
BATCH = 4
SEQ = 4096
DEPTH = 1
DEC_BATCH = 8
DEC_SEQ = 8192
PAST_LEN = 128

HEAD_DIM = 64
ATTN_HEADS = 8
ATTN_WIDTH = ATTN_HEADS * HEAD_DIM
DILATION_PATTERNS = ((128, 1), (512, 4), (2048, 16))
ATTN_BLOCK = 64
ROPE_THETA = 10000.0
GLA_HEADS = 4
GLA_DK = 64
GLA_DV = 128
GLA_KW = GLA_HEADS * GLA_DK
GLA_VW = GLA_HEADS * GLA_DV
GLA_GATE_RANK = 16
GLA_TAU = 16.0
GLA_CHUNK = 64
MIX_WIDTH = ATTN_WIDTH + GLA_VW
D_FF = -(-8 * D_MODEL // (3 * 256)) * 256
EPS = 1e-6
MASK_VALUE = -1e30
SPLIT_SIZES = (ATTN_WIDTH, ATTN_WIDTH, ATTN_WIDTH, GLA_KW, GLA_KW, GLA_VW, GLA_VW, GLA_GATE_RANK, GLA_GATE_RANK)
IN_COLS = sum(SPLIT_SIZES)

kernel_name = "hybrid_dilated_attn_gla_encoder"


def rmsnorm(x, g):
    xf = x.astype(jnp.float32)
    y = xf * lax.rsqrt(jnp.mean(xf * xf, axis=-1, keepdims=True) + EPS)
    return (y * g.astype(jnp.float32)).astype(x.dtype)


def rope(x):
    s, dh = x.shape[1], x.shape[-1]
    inv_freq = ROPE_THETA ** (-jnp.arange(0, dh, 2, dtype=jnp.float32) / dh)
    ang = jnp.arange(s, dtype=jnp.float32)[:, None] * inv_freq[None, :]
    cos = jnp.cos(ang)[None, :, None, :]
    sin = jnp.sin(ang)[None, :, None, :]
    xf = x.astype(jnp.float32)
    x1, x2 = xf[..., : dh // 2], xf[..., dh // 2:]
    return jnp.concatenate([x1 * cos - x2 * sin, x2 * cos + x1 * sin], axis=-1).astype(x.dtype)


def dilated_pattern(q, k, v, window, dilation):
    b, s, h, dh = q.shape
    d = dilation
    half = window // (2 * d)
    L = s // d
    nb = -(-L // ATTN_BLOCK)
    lp = nb * ATTN_BLOCK

    def to_sub(t):
        t = t.reshape(b, L, d, h, dh).transpose(0, 2, 3, 1, 4)
        return jnp.pad(t, ((0, 0), (0, 0), (0, 0), (0, lp - L), (0, 0)))

    def neighbours(t):
        t = jnp.pad(to_sub(t), ((0, 0), (0, 0), (0, 0), (ATTN_BLOCK, ATTN_BLOCK), (0, 0)))
        t = t.reshape(b, d, h, nb + 2, ATTN_BLOCK, dh)
        return jnp.concatenate([t[:, :, :, :-2], t[:, :, :, 1:-1], t[:, :, :, 2:]], axis=4)

    qb = to_sub(q).reshape(b, d, h, nb, ATTN_BLOCK, dh)
    kb = neighbours(k)
    vb = neighbours(v)
    qpos = jnp.arange(nb)[:, None] * ATTN_BLOCK + jnp.arange(ATTN_BLOCK)[None, :]
    kpos = (jnp.arange(nb)[:, None] - 1) * ATTN_BLOCK + jnp.arange(3 * ATTN_BLOCK)[None, :]
    dist = kpos[:, None, :] - qpos[:, :, None]
    valid = (jnp.abs(dist) <= half) & (kpos[:, None, :] >= 0) & (kpos[:, None, :] < L)
    scores = jnp.einsum('bdhnqe,bdhnke->bdhnqk', qb, kb,
                        preferred_element_type=jnp.float32) * (dh ** -0.5)
    scores = jnp.where(valid, scores, MASK_VALUE)
    lse = jax.nn.logsumexp(scores, axis=-1)
    p = jnp.exp(scores - lse[..., None])
    o = jnp.einsum('bdhnqk,bdhnke->bdhnqe', p, vb.astype(jnp.float32))
    o = o.reshape(b, d, h, lp, dh)[:, :, :, :L].transpose(0, 3, 1, 2, 4).reshape(b, s, h, dh)
    lse = lse.reshape(b, d, h, lp)[..., :L].transpose(0, 3, 1, 2).reshape(b, s, h)
    return o, lse


def dilated_mixture(q, k, v):
    outs, lses = [], []
    for window, dilation in DILATION_PATTERNS:
        o, l = dilated_pattern(q, k, v, window, dilation)
        outs.append(o)
        lses.append(l)
    w = jax.nn.softmax(jnp.stack(lses, axis=0), axis=0)
    return jnp.sum(w[..., None] * jnp.stack(outs, axis=0), axis=0)


def gla_chunked(q, k, v, log_a, include_diag):
    b, h, s, dk = q.shape
    dv = v.shape[-1]
    n = s // GLA_CHUNK
    q = q.reshape(b, h, n, GLA_CHUNK, dk)
    k = k.reshape(b, h, n, GLA_CHUNK, dk)
    v = v.reshape(b, h, n, GLA_CHUNK, dv)
    cum = jnp.cumsum(log_a.reshape(b, h, n, GLA_CHUNK, dk), axis=3)
    cum_last = cum[:, :, :, -1:, :]
    q_e = q * jnp.exp(cum)
    k_e = k * jnp.exp(-cum)
    k_end = k * jnp.exp(cum_last - cum)
    tri = jnp.tril(jnp.ones((GLA_CHUNK, GLA_CHUNK), dtype=bool), 0 if include_diag else -1)
    attn = jnp.where(tri, jnp.einsum('bhnce,bhnse->bhncs', q_e, k_e), 0.0)
    o_intra = jnp.einsum('bhncs,bhnsv->bhncv', attn, v)
    chunk_kv = jnp.einsum('bhnse,bhnsv->bhnev', k_end, v)
    chunk_decay = jnp.exp(cum_last[:, :, :, 0])

    def step(state, inp):
        dec, kv = inp
        return state * dec[..., None] + kv, state

    init = jnp.zeros((b, h, dk, dv), jnp.float32)
    _, states = lax.scan(step, init, (jnp.moveaxis(chunk_decay, 2, 0), jnp.moveaxis(chunk_kv, 2, 0)))
    states = jnp.moveaxis(states, 0, 2)
    o_inter = jnp.einsum('bhnce,bhnev->bhncv', q_e, states)
    return (o_intra + o_inter).reshape(b, h, s, dv)


def hybrid_mixer(hn, w_in, gate_up_fwd, gate_bias_fwd, gate_up_bwd, gate_bias_bwd, gla_norm_g, w_out):
    b, s, _ = hn.shape
    f32 = jnp.float32
    proj = hn @ w_in
    idx = np.cumsum(SPLIT_SIZES)[:-1].tolist()
    qa, ka, va, qg, kg, vg, og, rf, rb = jnp.split(proj, idx, axis=-1)

    qa = rope(qa.reshape(b, s, ATTN_HEADS, HEAD_DIM))
    ka = rope(ka.reshape(b, s, ATTN_HEADS, HEAD_DIM))
    va = va.reshape(b, s, ATTN_HEADS, HEAD_DIM)
    attn = dilated_mixture(qa, ka, va).reshape(b, s, ATTN_WIDTH)

    def to_bhsd(t):
        return t.reshape(b, s, GLA_HEADS, -1).transpose(0, 2, 1, 3).astype(f32)

    def log_decay(r, up, bias):
        z = jnp.einsum('bsr,rk->bsk', r.astype(f32), up.astype(f32)) + bias.astype(f32)
        return to_bhsd(jax.nn.log_sigmoid(z) / GLA_TAU)

    qg = to_bhsd(qg) * (GLA_DK ** -0.5)
    kg = to_bhsd(kg)
    vg = to_bhsd(vg)
    la_f = log_decay(rf, gate_up_fwd, gate_bias_fwd)
    la_b = log_decay(rb, gate_up_bwd, gate_bias_bwd)
    fwd = gla_chunked(qg, kg, vg, la_f, True)
    flip = lambda t: jnp.flip(t, axis=2)
    bwd = flip(gla_chunked(flip(qg), flip(kg), flip(vg), flip(la_b), False))
    o = (fwd + bwd).transpose(0, 2, 1, 3)
    o = o * lax.rsqrt(jnp.mean(o * o, axis=-1, keepdims=True) + EPS)
    o = o * gla_norm_g.astype(f32).reshape(GLA_HEADS, GLA_DV)
    gla = o.reshape(b, s, GLA_VW) * jax.nn.silu(og.astype(f32))

    mixed = jnp.concatenate([attn, gla], axis=-1).astype(hn.dtype)
    return mixed @ w_out


def swiglu(hn, w_gate, w_up, w_down):
    return (jax.nn.silu(hn @ w_gate) * (hn @ w_up)) @ w_down


def trunk(x, norm1_g, w_in, gate_up_fwd, gate_bias_fwd, gate_up_bwd, gate_bias_bwd,
          gla_norm_g, w_out, norm2_g, w_gate, w_up, w_down, final_norm_g):
    for l in range(DEPTH):
        x = x + hybrid_mixer(rmsnorm(x, norm1_g[l]), w_in[l], gate_up_fwd[l], gate_bias_fwd[l],
                             gate_up_bwd[l], gate_bias_bwd[l], gla_norm_g[l], w_out[l])
        x = x + swiglu(rmsnorm(x, norm2_g[l]), w_gate[l], w_up[l], w_down[l])
    return rmsnorm(x, final_norm_g)


def setup_inputs(seed: int = 0) -> dict:
    key = jax.random.key(seed)
    ks = jax.random.split(key, 16)
    f32 = jnp.float32
    nrm = lambda k, shape, scale: jax.random.normal(k, shape, f32) * scale
    return {
        "x_prompt": nrm(ks[0], (BATCH, SEQ, D_MODEL), 1.0),
        "x_sample": nrm(ks[1], (DEC_BATCH, DEC_SEQ, D_MODEL), 1.0),
        "norm1_g": 1.0 + nrm(ks[2], (DEPTH, D_MODEL), 0.02),
        "w_in": nrm(ks[3], (DEPTH, D_MODEL, IN_COLS), D_MODEL ** -0.5),
        "gate_up_fwd": nrm(ks[4], (DEPTH, GLA_GATE_RANK, GLA_KW), GLA_GATE_RANK ** -0.5),
        "gate_bias_fwd": nrm(ks[5], (DEPTH, GLA_KW), 0.1),
        "gate_up_bwd": nrm(ks[6], (DEPTH, GLA_GATE_RANK, GLA_KW), GLA_GATE_RANK ** -0.5),
        "gate_bias_bwd": nrm(ks[7], (DEPTH, GLA_KW), 0.1),
        "gla_norm_g": 1.0 + nrm(ks[8], (DEPTH, GLA_VW), 0.02),
        "w_out": nrm(ks[9], (DEPTH, MIX_WIDTH, D_MODEL), MIX_WIDTH ** -0.5),
        "norm2_g": 1.0 + nrm(ks[10], (DEPTH, D_MODEL), 0.02),
        "w_gate": nrm(ks[11], (DEPTH, D_MODEL, D_FF), D_MODEL ** -0.5),
        "w_up": nrm(ks[12], (DEPTH, D_MODEL, D_FF), D_MODEL ** -0.5),
        "w_down": nrm(ks[13], (DEPTH, D_FF, D_MODEL), D_FF ** -0.5),
        "final_norm_g": 1.0 + nrm(ks[14], (D_MODEL,), 0.02),
    }


def reference(x_prompt, x_sample, norm1_g, w_in, gate_up_fwd, gate_bias_fwd, gate_up_bwd, gate_bias_bwd,
              gla_norm_g, w_out, norm2_g, w_gate, w_up, w_down, final_norm_g):
    y_prompt = trunk(x_prompt, norm1_g, w_in, gate_up_fwd, gate_bias_fwd, gate_up_bwd, gate_bias_bwd,
                     gla_norm_g, w_out, norm2_g, w_gate, w_up, w_down, final_norm_g)
    y_sample = trunk(x_sample, norm1_g, w_in, gate_up_fwd, gate_bias_fwd, gate_up_bwd, gate_bias_bwd,
                     gla_norm_g, w_out, norm2_g, w_gate, w_up, w_down, final_norm_g)
    return (y_prompt, y_sample)
```

```python
import functools

import jax
import jax.numpy as jnp
from jax import lax
from jax.experimental import pallas as pl
from jax.experimental.pallas import tpu as pltpu

F32 = jnp.float32
BF16 = jnp.bfloat16

D_MODEL = 1024
HEAD_DIM = 64
ATTN_HEADS = 8
ATTN_WIDTH = ATTN_HEADS * HEAD_DIM
DILATIONS = (1, 4, 16)
HALF_WINDOW = 64
ROPE_THETA = 10000.0
GLA_HEADS = 4
GLA_DK = 64
GLA_DV = 128
GLA_KW = GLA_HEADS * GLA_DK
GLA_VW = GLA_HEADS * GLA_DV
GLA_GATE_RANK = 16
GLA_TAU = 16.0
GLA_CHUNK = 64
D_FF = 2816
EPS = 1e-6
MASK_VALUE = -1e30

LANES = 128
IN_COLS = 3 * ATTN_WIDTH + 2 * GLA_KW + 2 * GLA_VW + 2 * GLA_GATE_RANK
IN_COLS_PAD = 3200
GATE_COL0 = 3 * ATTN_WIDTH + 2 * GLA_KW + 2 * GLA_VW
VMEM_LIMIT = 56 * 1024 * 1024

TOKEN_TILE = 512
Q_BLOCK = 128
K_WINDOW = Q_BLOCK + 2 * HALF_WINDOW
SUPER_BLOCK = Q_BLOCK * DILATIONS[-1]
GLA_BLOCK = 512
FF_CHUNK = 1408


def _const_spec(shape):
    return pl.BlockSpec(shape, lambda *_: (0,) * len(shape), pipeline_mode=pl.Buffered(1))


def _inproj_kernel(x_ref, g_ref, w_ref, cos_ref, sina_ref, sinb_ref, up_ref, gb_ref,
                   qa_ref, ka_ref, va_ref, qg_ref, kg_ref, vg_ref, og_ref, laf_ref, lab_ref):
    x = x_ref[...]
    ms = jnp.mean(x * x, axis=-1, keepdims=True)
    hn = (x * lax.rsqrt(ms + EPS) * g_ref[...]).astype(BF16)

    def proj(col0, width):
        return jnp.dot(hn, w_ref[:, col0:col0 + width], preferred_element_type=F32)

    cos = cos_ref[...]
    sina = sina_ref[...]
    sinb = sinb_ref[...]

    def rope(t):
        out = []
        for j in range(t.shape[1] // LANES):
            s = t[:, j * LANES:(j + 1) * LANES]
            out.append(s * cos + pltpu.roll(s, HEAD_DIM // 2, 1) * sina
                       + pltpu.roll(s, LANES - HEAD_DIM // 2, 1) * sinb)
        return jnp.concatenate(out, axis=1)

    qa_ref[...] = rope(proj(0, ATTN_WIDTH)) * (HEAD_DIM ** -0.5)
    ka_ref[...] = rope(proj(ATTN_WIDTH, ATTN_WIDTH))
    va_ref[...] = proj(2 * ATTN_WIDTH, ATTN_WIDTH)
    c = 3 * ATTN_WIDTH
    qg_ref[...] = proj(c, GLA_KW) * (GLA_DK ** -0.5)
    kg_ref[...] = proj(c + GLA_KW, GLA_KW)
    vg_ref[...] = proj(c + 2 * GLA_KW, GLA_VW).astype(BF16)
    og_ref[...] = proj(c + 2 * GLA_KW + GLA_VW, GLA_VW).astype(BF16)
    r = proj(GATE_COL0, LANES).astype(BF16)
    z = jnp.dot(r, up_ref[...], preferred_element_type=F32) + gb_ref[...]
    la = (jnp.minimum(z, 0.0) - jnp.log(1.0 + jnp.exp(-jnp.abs(z)))) * (1.0 / GLA_TAU)
    laf_ref[...] = la[:, :GLA_KW]
    lab_ref[...] = la[:, GLA_KW:]


def _inproj(x2, seq, g, w, cos, sina, sinb, up, gb):
    t = x2.shape[0]
    tm = TOKEN_TILE
    n_pos = seq // tm
    row = lambda i: (i, 0)
    pos = lambda i: (i % n_pos, 0)
    out_shapes = [
        jax.ShapeDtypeStruct((t, ATTN_WIDTH), F32),
        jax.ShapeDtypeStruct((t, ATTN_WIDTH), F32),
        jax.ShapeDtypeStruct((t, ATTN_WIDTH), F32),
        jax.ShapeDtypeStruct((t, GLA_KW), F32),
        jax.ShapeDtypeStruct((t, GLA_KW), F32),
        jax.ShapeDtypeStruct((t, GLA_VW), BF16),
        jax.ShapeDtypeStruct((t, GLA_VW), BF16),
        jax.ShapeDtypeStruct((t, GLA_KW), F32),
        jax.ShapeDtypeStruct((t, GLA_KW), F32),
    ]
    return pl.pallas_call(
        _inproj_kernel,
        grid=(t // tm,),
        in_specs=[
            pl.BlockSpec((tm, D_MODEL), row),
            _const_spec((1, D_MODEL)),
            _const_spec((D_MODEL, IN_COLS_PAD)),
            pl.BlockSpec((tm, LANES), pos),
            pl.BlockSpec((tm, LANES), pos),
            pl.BlockSpec((tm, LANES), pos),
            _const_spec((LANES, 2 * GLA_KW)),
            _const_spec((1, 2 * GLA_KW)),
        ],
        out_specs=[pl.BlockSpec((tm, s.shape[1]), row) for s in out_shapes],
        out_shape=out_shapes,
        compiler_params=pltpu.CompilerParams(
            dimension_semantics=("parallel",), vmem_limit_bytes=VMEM_LIMIT),
        name="inproj",
    )(x2, g, w, cos, sina, sinb, up, gb)


def _attn_kernel(q_ref, k_ref, v_ref, o_ref, acc_ref, m0_ref, m1_ref, l0_ref, l1_ref, *, seq):
    lane = lax.broadcasted_iota(jnp.int32, (Q_BLOCK, LANES), 1)
    head0 = lane < HEAD_DIM
    col_minus_row = (lax.broadcasted_iota(jnp.int32, (Q_BLOCK, K_WINDOW), 1)
                     - lax.broadcasted_iota(jnp.int32, (Q_BLOCK, K_WINDOW), 0))
    m_refs = (m0_ref, m1_ref)
    l_refs = (l0_ref, l1_ref)

    def unit(base, d, cls, sub0, first, last):
        n_sub = seq // d
        win0 = jnp.clip(sub0 - HALF_WINDOW, 0, n_sub - K_WINDOW)
        if d > 1:
            q_rows = pl.ds(cls + d * sub0, Q_BLOCK, stride=d)
            k_rows = pl.ds(cls + d * win0, K_WINDOW, stride=d)
            s_rows = pl.ds(cls + d * sub0 - base, Q_BLOCK, stride=d)
        else:
            q_rows = pl.ds(pl.multiple_of(sub0, Q_BLOCK), Q_BLOCK)
            k_rows = pl.ds(pl.multiple_of(win0, HALF_WINDOW), K_WINDOW)
            s_rows = pl.ds(pl.multiple_of(sub0 - base, Q_BLOCK), Q_BLOCK)
        q = q_ref[q_rows, :]
        k = k_ref[k_rows, :].astype(BF16)
        v = v_ref[k_rows, :].astype(BF16)
        q2 = jnp.concatenate([jnp.where(head0, q, 0.0), jnp.where(head0, 0.0, q)],
                             axis=0).astype(BF16)
        s = lax.dot_general(q2, k, (((1,), (1,)), ((), ())), preferred_element_type=F32)
        valid = jnp.abs(col_minus_row + (win0 - sub0)) <= HALF_WINDOW
        ps, alphas, ls = [], [], []
        for h in range(2):
            sh = jnp.where(valid, s[h * Q_BLOCK:(h + 1) * Q_BLOCK], MASK_VALUE)
            m_loc = jnp.broadcast_to(jnp.max(sh, axis=-1, keepdims=True), (Q_BLOCK, LANES))
            if first:
                m_new = m_loc
            else:
                m_old = m_refs[h][s_rows, :]
                m_new = jnp.maximum(m_old, m_loc)
                alpha = jnp.exp(m_old - m_new)
                alphas.append(alpha)
            p = jnp.exp(sh - jnp.concatenate([m_new, m_new], axis=1))
            l_new = jnp.broadcast_to(jnp.sum(p, axis=-1, keepdims=True), (Q_BLOCK, LANES))
            if not first:
                l_new = l_refs[h][s_rows, :] * alpha + l_new
            if not last:
                m_refs[h][s_rows, :] = m_new
                l_refs[h][s_rows, :] = l_new
            ps.append(p.astype(BF16))
            ls.append(l_new)
        pv = jnp.dot(jnp.concatenate(ps, axis=0), v, preferred_element_type=F32)
        acc = jnp.where(head0, pv[:Q_BLOCK], pv[Q_BLOCK:])
        if not first:
            acc = acc_ref[s_rows, :] * jnp.where(head0, alphas[0], alphas[1]) + acc
        if last:
            o_ref[pl.ds(pl.multiple_of(sub0, Q_BLOCK), Q_BLOCK), :] = (
                acc / jnp.where(head0, ls[0], ls[1])).astype(o_ref.dtype)
        else:
            acc_ref[s_rows, :] = acc

    def super_block(sb, carry):
        base = pl.multiple_of(sb * SUPER_BLOCK, SUPER_BLOCK)
        n_pat = len(DILATIONS)
        for idx, d in enumerate(reversed(DILATIONS)):
            blocks_per_class = SUPER_BLOCK // d // Q_BLOCK

            def body(i, c, d=d, bpc=blocks_per_class, idx=idx):
                cls = i // bpc
                sub0 = sb * (SUPER_BLOCK // d) + (i % bpc) * Q_BLOCK
                unit(base, d, cls, sub0, idx == 0, idx == n_pat - 1)
                return c

            lax.fori_loop(0, SUPER_BLOCK // Q_BLOCK, body, 0)
        return carry

    lax.fori_loop(0, seq // SUPER_BLOCK, super_block, 0)


def _attention(q, k, v):
    b, seq, _ = q.shape
    spec = pl.BlockSpec((None, seq, LANES), lambda i, h: (i, 0, h))
    stat = pltpu.VMEM((SUPER_BLOCK, LANES), F32)
    return pl.pallas_call(
        functools.partial(_attn_kernel, seq=seq),
        grid=(b, ATTN_WIDTH // LANES),
        in_specs=[spec, spec, spec],
        out_specs=spec,
        out_shape=jax.ShapeDtypeStruct((b, seq, ATTN_WIDTH), BF16),
        scratch_shapes=[stat, stat, stat, stat, stat],
        compiler_params=pltpu.CompilerParams(
            dimension_semantics=("parallel", "parallel"), vmem_limit_bytes=VMEM_LIMIT),
        name="dilated_attn",
    )(q, k, v)


def _gla_kernel(qf_ref, kf_ref, vf_ref, laf_ref, qb_ref, kb_ref, vb_ref, lab_ref,
                of_ref, ob_ref, stf_ref, stb_ref):
    c = GLA_CHUNK

    @pl.when(pl.program_id(1) == 0)
    def _():
        stf_ref[...] = jnp.zeros_like(stf_ref)
        stb_ref[...] = jnp.zeros_like(stb_ref)

    row = lax.broadcasted_iota(jnp.int32, (c, c), 0)
    col = lax.broadcasted_iota(jnp.int32, (c, c), 1)
    tri_f = jnp.where(col <= row, 1.0, 0.0).astype(BF16)
    tri_b = jnp.where(col >= row, 1.0, 0.0).astype(BF16)
    row4 = lax.broadcasted_iota(jnp.int32, (GLA_HEADS * c, c), 0) % c
    col4 = lax.broadcasted_iota(jnp.int32, (GLA_HEADS * c, c), 1)
    keep_f = col4 <= row4
    keep_b = col4 > row4
    klane = lax.broadcasted_iota(jnp.int32, (c, GLA_KW), 1) // GLA_DK
    st_row = lax.broadcasted_iota(jnp.int32, (GLA_VW, GLA_KW), 0) // GLA_DV
    st_col = lax.broadcasted_iota(jnp.int32, (GLA_VW, GLA_KW), 1) // GLA_DK
    own_head = st_row == st_col

    def chunk(q, k, v, la, st_ref, tri, keep, last_row):
        la_hi = la.astype(BF16)
        la_lo = (la - la_hi.astype(F32)).astype(BF16)
        cum = (jnp.dot(tri, la_hi, preferred_element_type=F32)
               + jnp.dot(tri, la_lo, preferred_element_type=F32))
        cum_last = cum[last_row:last_row + 1, :]
        q_e = q * jnp.exp(cum)
        k_e = (k * jnp.exp(-cum)).astype(BF16)
        k_end = (k * jnp.exp(cum_last - cum)).astype(BF16)
        q_heads = jnp.concatenate(
            [jnp.where(klane == h, q_e, 0.0) for h in range(GLA_HEADS)], axis=0).astype(BF16)
        attn = lax.dot_general(q_heads, k_e, (((1,), (1,)), ((), ())),
                               preferred_element_type=F32)
        attn = jnp.where(keep, attn, 0.0).astype(BF16)
        o_all = jnp.dot(attn, v, preferred_element_type=F32)
        o_intra = jnp.concatenate(
            [o_all[h * c:(h + 1) * c, h * GLA_DV:(h + 1) * GLA_DV] for h in range(GLA_HEADS)],
            axis=1)
        st = st_ref[...]
        o_inter = lax.dot_general(q_e.astype(BF16), st.astype(BF16), (((1,), (1,)), ((), ())),
                                  preferred_element_type=F32)
        kv_t = lax.dot_general(v, k_end, (((0,), (0,)), ((), ())), preferred_element_type=F32)
        st_ref[...] = st * jnp.exp(cum_last) + jnp.where(own_head, kv_t, 0.0)
        return o_intra + o_inter

    n_chunks = GLA_BLOCK // c

    def body(i, carry):
        rf = pl.ds(pl.multiple_of(i * c, c), c)
        of_ref[rf, :] = chunk(qf_ref[rf, :], kf_ref[rf, :], vf_ref[rf, :], laf_ref[rf, :],
                              stf_ref, tri_f, keep_f, c - 1)
        rb = pl.ds(pl.multiple_of((n_chunks - 1 - i) * c, c), c)
        ob_ref[rb, :] = chunk(qb_ref[rb, :], kb_ref[rb, :], vb_ref[rb, :], lab_ref[rb, :],
                              stb_ref, tri_b, keep_b, 0)
        return carry

    lax.fori_loop(0, n_chunks, body, 0)


def _gla(q, k, v, la_f, la_b):
    b, seq, _ = q.shape
    n = seq // GLA_BLOCK
    fwd = lambda w: pl.BlockSpec((None, GLA_BLOCK, w), lambda i, j: (i, j, 0))
    bwd = lambda w: pl.BlockSpec((None, GLA_BLOCK, w), lambda i, j: (i, n - 1 - j, 0))
    out = jax.ShapeDtypeStruct((b, seq, GLA_VW), F32)
    state = pltpu.VMEM((GLA_VW, GLA_KW), F32)
    return pl.pallas_call(
        _gla_kernel,
        grid=(b, n),
        in_specs=[fwd(GLA_KW), fwd(GLA_KW), fwd(GLA_VW), fwd(GLA_KW),
                  bwd(GLA_KW), bwd(GLA_KW), bwd(GLA_VW), bwd(GLA_KW)],
        out_specs=[fwd(GLA_VW), bwd(GLA_VW)],
        out_shape=[out, out],
        scratch_shapes=[state, state],
        compiler_params=pltpu.CompilerParams(
            dimension_semantics=("parallel", "arbitrary"), vmem_limit_bytes=VMEM_LIMIT),
        name="gla",
    )(q, k, v, la_f, q, k, v, la_b)


def _out_ffn_kernel(x_ref, attn_ref, of_ref, ob_ref, og_ref, gn_ref, wo_ref, g2_ref,
                    wg_ref, wu_ref, wd_ref, gf_ref, y_ref):
    def rms(t, g):
        return t * lax.rsqrt(jnp.mean(t * t, axis=-1, keepdims=True) + EPS) * g

    o = of_ref[...] + ob_ref[...]
    og = og_ref[...].astype(F32)
    gate = og * (1.0 / (1.0 + jnp.exp(-og)))
    heads = []
    for h in range(GLA_HEADS):
        sl = slice(h * GLA_DV, (h + 1) * GLA_DV)
        heads.append(rms(o[:, sl], gn_ref[:, sl]) * gate[:, sl])
    mixed = jnp.concatenate([attn_ref[...]] + [t.astype(BF16) for t in heads], axis=1)
    x1 = x_ref[...] + jnp.dot(mixed, wo_ref[...], preferred_element_type=F32)
    h2 = rms(x1, g2_ref[...]).astype(BF16)
    ff = jnp.zeros_like(x1)
    for j in range(D_FF // FF_CHUNK):
        cs = slice(j * FF_CHUNK, (j + 1) * FF_CHUNK)
        a = jnp.dot(h2, wg_ref[:, cs], preferred_element_type=F32)
        u = jnp.dot(h2, wu_ref[:, cs], preferred_element_type=F32)
        act = (a * (1.0 / (1.0 + jnp.exp(-a))) * u).astype(BF16)
        ff = ff + jnp.dot(act, wd_ref[cs, :], preferred_element_type=F32)
    y_ref[...] = rms(x1 + ff, gf_ref[...])


def _out_ffn(x2, attn, o_f, o_b, og, gn, wo, g2, wg, wu, wd, gf):
    t = x2.shape[0]
    tm = TOKEN_TILE
    row = lambda i: (i, 0)
    return pl.pallas_call(
        _out_ffn_kernel,
        grid=(t // tm,),
        in_specs=[
            pl.BlockSpec((tm, D_MODEL), row),
            pl.BlockSpec((tm, ATTN_WIDTH), row),
            pl.BlockSpec((tm, GLA_VW), row),
            pl.BlockSpec((tm, GLA_VW), row),
            pl.BlockSpec((tm, GLA_VW), row),
            _const_spec((1, GLA_VW)),
            _const_spec((D_MODEL, D_MODEL)),
            _const_spec((1, D_MODEL)),
            _const_spec((D_MODEL, D_FF)),
            _const_spec((D_MODEL, D_FF)),
            _const_spec((D_FF, D_MODEL)),
            _const_spec((1, D_MODEL)),
        ],
        out_specs=pl.BlockSpec((tm, D_MODEL), row),
        out_shape=jax.ShapeDtypeStruct((t, D_MODEL), F32),
        compiler_params=pltpu.CompilerParams(
            dimension_semantics=("parallel",), vmem_limit_bytes=VMEM_LIMIT),
        name="out_ffn",
    )(x2, attn, o_f, o_b, og, gn, wo, g2, wg, wu, wd, gf)


def _rope_tables(seq):
    inv_freq = ROPE_THETA ** (-jnp.arange(0, HEAD_DIM, 2, dtype=F32) / HEAD_DIM)
    ang = jnp.arange(seq, dtype=F32)[:, None] * inv_freq[None, :]
    cos = jnp.tile(jnp.cos(ang), (1, LANES // (HEAD_DIM // 2)))
    sin = jnp.tile(jnp.sin(ang), (1, LANES // (HEAD_DIM // 2)))
    second_half = (jnp.arange(LANES) % HEAD_DIM) >= HEAD_DIM // 2
    sina = jnp.where(second_half[None, :], sin, 0.0)
    sinb = jnp.where(second_half[None, :], 0.0, -sin)
    return cos, sina, sinb


def _trunk(x, p):
    b, seq, _ = x.shape
    assert seq % SUPER_BLOCK == 0 and seq // DILATIONS[-1] >= K_WINDOW
    assert seq % GLA_BLOCK == 0 and (b * seq) % TOKEN_TILE == 0 and seq % TOKEN_TILE == 0
    x2 = x.reshape(b * seq, D_MODEL)
    cos, sina, sinb = _rope_tables(seq)
    qa, ka, va, qg, kg, vg, og, la_f, la_b = _inproj(
        x2, seq, p["g1"], p["w_in"], cos, sina, sinb, p["gate_up"], p["gate_bias"])
    r3 = lambda a: a.reshape(b, seq, a.shape[-1])
    attn = _attention(r3(qa), r3(ka), r3(va))
    o_f, o_b = _gla(r3(qg), r3(kg), r3(vg), r3(la_f), r3(la_b))
    r2 = lambda a: a.reshape(b * seq, a.shape[-1])
    y = _out_ffn(x2, r2(attn), r2(o_f), r2(o_b), og, p["gn"], p["w_out"], p["g2"],
                 p["w_gate"], p["w_up"], p["w_down"], p["gf"])
    return y.reshape(b, seq, D_MODEL)


def kernel(x_prompt, x_sample, norm1_g, w_in, gate_up_fwd, gate_bias_fwd, gate_up_bwd, gate_bias_bwd,
           gla_norm_g, w_out, norm2_g, w_gate, w_up, w_down, final_norm_g):
    assert norm1_g.shape[0] == 1, "single layer"
    w_in_p = jnp.pad(w_in[0], ((0, 0), (0, IN_COLS_PAD - IN_COLS))).astype(BF16)
    up = jnp.zeros((LANES, 2 * GLA_KW), F32)
    up = up.at[:GLA_GATE_RANK, :GLA_KW].set(gate_up_fwd[0])
    up = up.at[GLA_GATE_RANK:2 * GLA_GATE_RANK, GLA_KW:].set(gate_up_bwd[0])
    p = {
        "g1": norm1_g[0][None, :],
        "w_in": w_in_p,
        "gate_up": up.astype(BF16),
        "gate_bias": jnp.concatenate([gate_bias_fwd[0], gate_bias_bwd[0]])[None, :],
        "gn": gla_norm_g[0][None, :],
        "w_out": w_out[0].astype(BF16),
        "g2": norm2_g[0][None, :],
        "w_gate": w_gate[0].astype(BF16),
        "w_up": w_up[0].astype(BF16),
        "w_down": w_down[0].astype(BF16),
        "gf": final_norm_g[None, :],
    }
    return _trunk(x_prompt, p), _trunk(x_sample, p)
```

```python
import functools

import jax
import jax.numpy as jnp
from jax import lax
from jax.experimental import pallas as pl
from jax.experimental.pallas import tpu as pltpu

F32 = jnp.float32
BF16 = jnp.bfloat16

D_MODEL = 1024
HEAD_DIM = 64
ATTN_HEADS = 8
ATTN_WIDTH = ATTN_HEADS * HEAD_DIM
DILATIONS = (1, 4, 16)
HALF_WINDOW = 64
ROPE_THETA = 10000.0
GLA_HEADS = 4
GLA_DK = 64
GLA_DV = 128
GLA_KW = GLA_HEADS * GLA_DK
GLA_VW = GLA_HEADS * GLA_DV
GLA_GATE_RANK = 16
GLA_TAU = 16.0
GLA_CHUNK = 64
D_FF = 2816
EPS = 1e-6
MASK_VALUE = -1e30

LANES = 128
IN_COLS = 3 * ATTN_WIDTH + 2 * GLA_KW + 2 * GLA_VW + 2 * GLA_GATE_RANK
IN_COLS_PAD = 3200
GATE_COL0 = 3 * ATTN_WIDTH + 2 * GLA_KW + 2 * GLA_VW
VMEM_LIMIT = 56 * 1024 * 1024

TOKEN_TILE = 512
Q_BLOCK = 128
K_WINDOW = Q_BLOCK + 2 * HALF_WINDOW
SUPER_BLOCK = Q_BLOCK * DILATIONS[-1]
ATTN_UNROLL = 8
GLA_BLOCK = 512
GLA_UNROLL = 4
FF_CHUNK = 1408


def _const_spec(shape):
    return pl.BlockSpec(shape, lambda *_: (0,) * len(shape), pipeline_mode=pl.Buffered(1))


def _inproj_kernel(x_ref, g_ref, w_ref, cos_ref, sina_ref, sinb_ref, up_ref, gb_ref,
                   qa_ref, ka_ref, va_ref, qg_ref, kg_ref, vg_ref, og_ref, laf_ref, lab_ref):
    x = x_ref[...]
    ms = jnp.mean(x * x, axis=-1, keepdims=True)
    hn = (x * lax.rsqrt(ms + EPS) * g_ref[...]).astype(BF16)

    def proj(col0, width):
        return jnp.dot(hn, w_ref[:, col0:col0 + width], preferred_element_type=F32)

    cos = cos_ref[...]
    sina = sina_ref[...]
    sinb = sinb_ref[...]

    def rope(t):
        out = []
        for j in range(t.shape[1] // LANES):
            s = t[:, j * LANES:(j + 1) * LANES]
            out.append(s * cos + pltpu.roll(s, HEAD_DIM // 2, 1) * sina
                       + pltpu.roll(s, LANES - HEAD_DIM // 2, 1) * sinb)
        return jnp.concatenate(out, axis=1)

    qa_ref[...] = rope(proj(0, ATTN_WIDTH)) * (HEAD_DIM ** -0.5)
    ka_ref[...] = rope(proj(ATTN_WIDTH, ATTN_WIDTH))
    va_ref[...] = proj(2 * ATTN_WIDTH, ATTN_WIDTH)
    c = 3 * ATTN_WIDTH
    qg_ref[...] = proj(c, GLA_KW) * (GLA_DK ** -0.5)
    kg_ref[...] = proj(c + GLA_KW, GLA_KW)
    vg_ref[...] = proj(c + 2 * GLA_KW, GLA_VW).astype(BF16)
    og_ref[...] = proj(c + 2 * GLA_KW + GLA_VW, GLA_VW).astype(BF16)
    r = proj(GATE_COL0, LANES).astype(BF16)
    z = jnp.dot(r, up_ref[...], preferred_element_type=F32) + gb_ref[...]
    la = (jnp.minimum(z, 0.0) - jnp.log(1.0 + jnp.exp(-jnp.abs(z)))) * (1.0 / GLA_TAU)
    laf_ref[...] = la[:, :GLA_KW]
    lab_ref[...] = la[:, GLA_KW:]


def _inproj(x2, seq, g, w, cos, sina, sinb, up, gb):
    t = x2.shape[0]
    tm = TOKEN_TILE
    n_pos = seq // tm
    row = lambda i: (i, 0)
    pos = lambda i: (i % n_pos, 0)
    out_shapes = [
        jax.ShapeDtypeStruct((t, ATTN_WIDTH), F32),
        jax.ShapeDtypeStruct((t, ATTN_WIDTH), F32),
        jax.ShapeDtypeStruct((t, ATTN_WIDTH), F32),
        jax.ShapeDtypeStruct((t, GLA_KW), F32),
        jax.ShapeDtypeStruct((t, GLA_KW), F32),
        jax.ShapeDtypeStruct((t, GLA_VW), BF16),
        jax.ShapeDtypeStruct((t, GLA_VW), BF16),
        jax.ShapeDtypeStruct((t, GLA_KW), F32),
        jax.ShapeDtypeStruct((t, GLA_KW), F32),
    ]
    return pl.pallas_call(
        _inproj_kernel,
        grid=(t // tm,),
        in_specs=[
            pl.BlockSpec((tm, D_MODEL), row),
            _const_spec((1, D_MODEL)),
            _const_spec((D_MODEL, IN_COLS_PAD)),
            pl.BlockSpec((tm, LANES), pos),
            pl.BlockSpec((tm, LANES), pos),
            pl.BlockSpec((tm, LANES), pos),
            _const_spec((LANES, 2 * GLA_KW)),
            _const_spec((1, 2 * GLA_KW)),
        ],
        out_specs=[pl.BlockSpec((tm, s.shape[1]), row) for s in out_shapes],
        out_shape=out_shapes,
        compiler_params=pltpu.CompilerParams(
            dimension_semantics=("parallel",), vmem_limit_bytes=VMEM_LIMIT),
        name="inproj",
    )(x2, g, w, cos, sina, sinb, up, gb)


def _attn_kernel(q_ref, k_ref, v_ref, o_ref, o4_ref, lse4_ref, o16_ref, lse16_ref, *, seq):
    lane = lax.broadcasted_iota(jnp.int32, (Q_BLOCK, LANES), 1)
    head0 = lane < HEAD_DIM
    col_minus_row = (lax.broadcasted_iota(jnp.int32, (Q_BLOCK, K_WINDOW), 1)
                     - lax.broadcasted_iota(jnp.int32, (Q_BLOCK, K_WINDOW), 0))
    ones = jnp.ones((K_WINDOW, LANES), BF16)

    def pattern(d, cls, sub0):
        n_sub = seq // d
        win0 = jnp.clip(sub0 - HALF_WINDOW, 0, n_sub - K_WINDOW)
        if d > 1:
            q_rows = pl.ds(cls + d * sub0, Q_BLOCK, stride=d)
            k_rows = pl.ds(cls + d * win0, K_WINDOW, stride=d)
        else:
            q_rows = pl.ds(pl.multiple_of(sub0, Q_BLOCK), Q_BLOCK)
            k_rows = pl.ds(pl.multiple_of(win0, HALF_WINDOW), K_WINDOW)
        q = q_ref[q_rows, :]
        k = k_ref[k_rows, :].astype(BF16)
        v = jnp.concatenate([v_ref[k_rows, :].astype(BF16), ones], axis=1)
        q2 = jnp.concatenate([jnp.where(head0, q, 0.0), jnp.where(head0, 0.0, q)],
                             axis=0).astype(BF16)
        s = lax.dot_general(q2, k, (((1,), (1,)), ((), ())), preferred_element_type=F32)
        valid = jnp.abs(col_minus_row + (win0 - sub0)) <= HALF_WINDOW
        ps, ms = [], []
        for h in range(2):
            sh = jnp.where(valid, s[h * Q_BLOCK:(h + 1) * Q_BLOCK], MASK_VALUE)
            m = jnp.max(sh, axis=-1, keepdims=True)
            ms.append(jnp.broadcast_to(m, (Q_BLOCK, LANES)))
            ps.append(jnp.exp(sh - m).astype(BF16))
        pv = jnp.dot(jnp.concatenate(ps, axis=0), v, preferred_element_type=F32)
        l = jnp.where(head0, pv[:Q_BLOCK, LANES:], pv[Q_BLOCK:, LANES:])
        out = jnp.where(head0, pv[:Q_BLOCK, :LANES], pv[Q_BLOCK:, :LANES]) / l
        return out, jnp.where(head0, ms[0], ms[1]) + jnp.log(l)

    def super_block(sb, carry):
        base = pl.multiple_of(sb * SUPER_BLOCK, SUPER_BLOCK)

        for d, o_d, lse_d in ((16, o16_ref, lse16_ref), (4, o4_ref, lse4_ref)):
            blocks_per_class = SUPER_BLOCK // d // Q_BLOCK

            def park(i, c, d=d, bpc=blocks_per_class, o_d=o_d, lse_d=lse_d):
                cls = i // bpc
                blk = (i % bpc) * Q_BLOCK
                out, lse = pattern(d, cls, sb * (SUPER_BLOCK // d) + blk)
                rows = pl.ds(cls + d * blk, Q_BLOCK, stride=d)
                o_d[rows, :] = out
                lse_d[rows, :] = lse
                return c

            lax.fori_loop(0, SUPER_BLOCK // Q_BLOCK, park, 0, unroll=ATTN_UNROLL)

        def merge(i, c):
            blk = pl.multiple_of(i * Q_BLOCK, Q_BLOCK)
            out1, lse1 = pattern(1, 0, base + blk)
            rows = pl.ds(blk, Q_BLOCK)
            lse4 = lse4_ref[rows, :]
            lse16 = lse16_ref[rows, :]
            top = jnp.maximum(jnp.maximum(lse1, lse4), lse16)
            w1 = jnp.exp(lse1 - top)
            w4 = jnp.exp(lse4 - top)
            w16 = jnp.exp(lse16 - top)
            mix = (w1 * out1 + w4 * o4_ref[rows, :] + w16 * o16_ref[rows, :]) / (w1 + w4 + w16)
            o_ref[pl.ds(base + blk, Q_BLOCK), :] = mix.astype(o_ref.dtype)
            return c

        lax.fori_loop(0, SUPER_BLOCK // Q_BLOCK, merge, 0, unroll=ATTN_UNROLL)
        return carry

    lax.fori_loop(0, seq // SUPER_BLOCK, super_block, 0)


def _attention(q, k, v):
    b, seq, _ = q.shape
    spec = pl.BlockSpec((None, seq, LANES), lambda i, h: (i, 0, h))
    stat = pltpu.VMEM((SUPER_BLOCK, LANES), F32)
    return pl.pallas_call(
        functools.partial(_attn_kernel, seq=seq),
        grid=(b, ATTN_WIDTH // LANES),
        in_specs=[spec, spec, spec],
        out_specs=spec,
        out_shape=jax.ShapeDtypeStruct((b, seq, ATTN_WIDTH), BF16),
        scratch_shapes=[stat, stat, stat, stat],
        compiler_params=pltpu.CompilerParams(
            dimension_semantics=("parallel", "parallel"), vmem_limit_bytes=VMEM_LIMIT),
        name="dilated_attn",
    )(q, k, v)


def _gla_kernel(qf_ref, kf_ref, vf_ref, laf_ref, qb_ref, kb_ref, vb_ref, lab_ref,
                of_ref, ob_ref, stf_ref, stb_ref):
    c = GLA_CHUNK

    @pl.when(pl.program_id(1) == 0)
    def _():
        stf_ref[...] = jnp.zeros_like(stf_ref)
        stb_ref[...] = jnp.zeros_like(stb_ref)

    row = lax.broadcasted_iota(jnp.int32, (c, c), 0)
    col = lax.broadcasted_iota(jnp.int32, (c, c), 1)
    tri_f = jnp.where(col <= row, 1.0, 0.0).astype(BF16)
    tri_b = jnp.where(col >= row, 1.0, 0.0).astype(BF16)
    row4 = lax.broadcasted_iota(jnp.int32, (GLA_HEADS * c, c), 0) % c
    col4 = lax.broadcasted_iota(jnp.int32, (GLA_HEADS * c, c), 1)
    keep_f = col4 <= row4
    keep_b = col4 > row4
    klane = lax.broadcasted_iota(jnp.int32, (c, GLA_KW), 1) // GLA_DK
    nt = (((1,), (1,)), ((), ()))
    tn = (((0,), (0,)), ((), ()))

    def by_head(t):
        return jnp.concatenate(
            [jnp.where(klane == h, t, 0.0) for h in range(GLA_HEADS)], axis=0).astype(BF16)

    def chunk(q, k, v, la, st_ref, tri, keep, last_row):
        la_hi = la.astype(BF16)
        la_lo = (la - la_hi.astype(F32)).astype(BF16)
        cum = (jnp.dot(tri, la_hi, preferred_element_type=F32)
               + jnp.dot(tri, la_lo, preferred_element_type=F32))
        cum_last = cum[last_row:last_row + 1, :]
        q_e = q * jnp.exp(cum)
        k_e = (k * jnp.exp(-cum)).astype(BF16)
        k_end = k * jnp.exp(cum_last - cum)
        q_heads = by_head(q_e)
        attn = lax.dot_general(q_heads, k_e, nt, preferred_element_type=F32)
        attn = jnp.where(keep, attn, 0.0).astype(BF16)
        st = st_ref[...]
        o_state = lax.dot_general(q_heads, st.astype(BF16), nt, preferred_element_type=F32)
        o = jnp.concatenate(
            [jnp.dot(attn[h * c:(h + 1) * c], v[:, h * GLA_DV:(h + 1) * GLA_DV],
                     preferred_element_type=F32) + o_state[h * c:(h + 1) * c]
             for h in range(GLA_HEADS)], axis=1)
        v_heads = jnp.concatenate(
            [v[:, h * GLA_DV:(h + 1) * GLA_DV] for h in range(GLA_HEADS)], axis=0)
        kv_t = lax.dot_general(v_heads, by_head(k_end), tn, preferred_element_type=F32)
        st_ref[...] = st * jnp.exp(cum_last) + kv_t
        return o

    n_chunks = GLA_BLOCK // c

    def body(i, carry):
        rf = pl.ds(pl.multiple_of(i * c, c), c)
        of_ref[rf, :] = chunk(qf_ref[rf, :], kf_ref[rf, :], vf_ref[rf, :], laf_ref[rf, :],
                              stf_ref, tri_f, keep_f, c - 1)
        rb = pl.ds(pl.multiple_of((n_chunks - 1 - i) * c, c), c)
        ob_ref[rb, :] = chunk(qb_ref[rb, :], kb_ref[rb, :], vb_ref[rb, :], lab_ref[rb, :],
                              stb_ref, tri_b, keep_b, 0)
        return carry

    lax.fori_loop(0, n_chunks, body, 0, unroll=GLA_UNROLL)


def _gla(q, k, v, la_f, la_b):
    b, seq, _ = q.shape
    n = seq // GLA_BLOCK
    fwd = lambda w: pl.BlockSpec((None, GLA_BLOCK, w), lambda i, j: (i, j, 0))
    bwd = lambda w: pl.BlockSpec((None, GLA_BLOCK, w), lambda i, j: (i, n - 1 - j, 0))
    out = jax.ShapeDtypeStruct((b, seq, GLA_VW), F32)
    state = pltpu.VMEM((GLA_DV, GLA_KW), F32)
    return pl.pallas_call(
        _gla_kernel,
        grid=(b, n),
        in_specs=[fwd(GLA_KW), fwd(GLA_KW), fwd(GLA_VW), fwd(GLA_KW),
                  bwd(GLA_KW), bwd(GLA_KW), bwd(GLA_VW), bwd(GLA_KW)],
        out_specs=[fwd(GLA_VW), bwd(GLA_VW)],
        out_shape=[out, out],
        scratch_shapes=[state, state],
        compiler_params=pltpu.CompilerParams(
            dimension_semantics=("parallel", "arbitrary"), vmem_limit_bytes=VMEM_LIMIT),
        name="gla",
    )(q, k, v, la_f, q, k, v, la_b)


def _out_ffn_kernel(x_ref, attn_ref, of_ref, ob_ref, og_ref, gn_ref, wo_ref, g2_ref,
                    wg_ref, wu_ref, wd_ref, gf_ref, y_ref):
    def rms(t, g):
        return t * lax.rsqrt(jnp.mean(t * t, axis=-1, keepdims=True) + EPS) * g

    o = of_ref[...] + ob_ref[...]
    og = og_ref[...].astype(F32)
    gate = og * (1.0 / (1.0 + jnp.exp(-og)))
    heads = []
    for h in range(GLA_HEADS):
        sl = slice(h * GLA_DV, (h + 1) * GLA_DV)
        heads.append(rms(o[:, sl], gn_ref[:, sl]) * gate[:, sl])
    mixed = jnp.concatenate([attn_ref[...]] + [t.astype(BF16) for t in heads], axis=1)
    x1 = x_ref[...] + jnp.dot(mixed, wo_ref[...], preferred_element_type=F32)
    h2 = rms(x1, g2_ref[...]).astype(BF16)
    ff = jnp.zeros_like(x1)
    for j in range(D_FF // FF_CHUNK):
        cs = slice(j * FF_CHUNK, (j + 1) * FF_CHUNK)
        a = jnp.dot(h2, wg_ref[:, cs], preferred_element_type=F32)
        u = jnp.dot(h2, wu_ref[:, cs], preferred_element_type=F32)
        act = (a * (1.0 / (1.0 + jnp.exp(-a))) * u).astype(BF16)
        ff = ff + jnp.dot(act, wd_ref[cs, :], preferred_element_type=F32)
    y_ref[...] = rms(x1 + ff, gf_ref[...])


def _out_ffn(x2, attn, o_f, o_b, og, gn, wo, g2, wg, wu, wd, gf):
    t = x2.shape[0]
    tm = TOKEN_TILE
    row = lambda i: (i, 0)
    return pl.pallas_call(
        _out_ffn_kernel,
        grid=(t // tm,),
        in_specs=[
            pl.BlockSpec((tm, D_MODEL), row),
            pl.BlockSpec((tm, ATTN_WIDTH), row),
            pl.BlockSpec((tm, GLA_VW), row),
            pl.BlockSpec((tm, GLA_VW), row),
            pl.BlockSpec((tm, GLA_VW), row),
            _const_spec((1, GLA_VW)),
            _const_spec((D_MODEL, D_MODEL)),
            _const_spec((1, D_MODEL)),
            _const_spec((D_MODEL, D_FF)),
            _const_spec((D_MODEL, D_FF)),
            _const_spec((D_FF, D_MODEL)),
            _const_spec((1, D_MODEL)),
        ],
        out_specs=pl.BlockSpec((tm, D_MODEL), row),
        out_shape=jax.ShapeDtypeStruct((t, D_MODEL), F32),
        compiler_params=pltpu.CompilerParams(
            dimension_semantics=("parallel",), vmem_limit_bytes=VMEM_LIMIT),
        name="out_ffn",
    )(x2, attn, o_f, o_b, og, gn, wo, g2, wg, wu, wd, gf)


def _rope_tables(seq):
    inv_freq = ROPE_THETA ** (-jnp.arange(0, HEAD_DIM, 2, dtype=F32) / HEAD_DIM)
    ang = jnp.arange(seq, dtype=F32)[:, None] * inv_freq[None, :]
    cos = jnp.tile(jnp.cos(ang), (1, LANES // (HEAD_DIM // 2)))
    sin = jnp.tile(jnp.sin(ang), (1, LANES // (HEAD_DIM // 2)))
    second_half = (jnp.arange(LANES) % HEAD_DIM) >= HEAD_DIM // 2
    sina = jnp.where(second_half[None, :], sin, 0.0)
    sinb = jnp.where(second_half[None, :], 0.0, -sin)
    return cos, sina, sinb


def _trunk(x, p):
    b, seq, _ = x.shape
    assert seq % SUPER_BLOCK == 0 and seq // DILATIONS[-1] >= K_WINDOW
    assert seq % GLA_BLOCK == 0 and (b * seq) % TOKEN_TILE == 0 and seq % TOKEN_TILE == 0
    x2 = x.reshape(b * seq, D_MODEL)
    cos, sina, sinb = _rope_tables(seq)
    qa, ka, va, qg, kg, vg, og, la_f, la_b = _inproj(
        x2, seq, p["g1"], p["w_in"], cos, sina, sinb, p["gate_up"], p["gate_bias"])
    r3 = lambda a: a.reshape(b, seq, a.shape[-1])
    attn = _attention(r3(qa), r3(ka), r3(va))
    o_f, o_b = _gla(r3(qg), r3(kg), r3(vg), r3(la_f), r3(la_b))
    r2 = lambda a: a.reshape(b * seq, a.shape[-1])
    y = _out_ffn(x2, r2(attn), r2(o_f), r2(o_b), og, p["gn"], p["w_out"], p["g2"],
                 p["w_gate"], p["w_up"], p["w_down"], p["gf"])
    return y.reshape(b, seq, D_MODEL)


def kernel(x_prompt, x_sample, norm1_g, w_in, gate_up_fwd, gate_bias_fwd, gate_up_bwd, gate_bias_bwd,
           gla_norm_g, w_out, norm2_g, w_gate, w_up, w_down, final_norm_g):
    assert norm1_g.shape[0] == 1, "single layer"
    w_in_p = jnp.pad(w_in[0], ((0, 0), (0, IN_COLS_PAD - IN_COLS))).astype(BF16)
    up = jnp.zeros((LANES, 2 * GLA_KW), F32)
    up = up.at[:GLA_GATE_RANK, :GLA_KW].set(gate_up_fwd[0])
    up = up.at[GLA_GATE_RANK:2 * GLA_GATE_RANK, GLA_KW:].set(gate_up_bwd[0])
    p = {
        "g1": norm1_g[0][None, :],
        "w_in": w_in_p,
        "gate_up": up.astype(BF16),
        "gate_bias": jnp.concatenate([gate_bias_fwd[0], gate_bias_bwd[0]])[None, :],
        "gn": gla_norm_g[0][None, :],
        "w_out": w_out[0].astype(BF16),
        "g2": norm2_g[0][None, :],
        "w_gate": w_gate[0].astype(BF16),
        "w_up": w_up[0].astype(BF16),
        "w_down": w_down[0].astype(BF16),
        "gf": final_norm_g[None, :],
    }
    return _trunk(x_prompt, p), _trunk(x_sample, p)
```

```python
import functools

import jax
import jax.numpy as jnp
from jax import lax
from jax.experimental import pallas as pl
from jax.experimental.pallas import tpu as pltpu

F32 = jnp.float32
BF16 = jnp.bfloat16

D_MODEL = 1024
HEAD_DIM = 64
ATTN_HEADS = 8
ATTN_WIDTH = ATTN_HEADS * HEAD_DIM
DILATIONS = (1, 4, 16)
HALF_WINDOW = 64
ROPE_THETA = 10000.0
GLA_HEADS = 4
GLA_DK = 64
GLA_DV = 128
GLA_KW = GLA_HEADS * GLA_DK
GLA_VW = GLA_HEADS * GLA_DV
GLA_GATE_RANK = 16
GLA_TAU = 16.0
GLA_CHUNK = 64
D_FF = 2816
EPS = 1e-6
MASK_VALUE = -1e30
LOG2_E = 1.4426950408889634

LANES = 128
IN_COLS = 3 * ATTN_WIDTH + 2 * GLA_KW + 2 * GLA_VW + 2 * GLA_GATE_RANK
IN_COLS_PAD = 3200
GATE_COL0 = 3 * ATTN_WIDTH + 2 * GLA_KW + 2 * GLA_VW
VMEM_LIMIT = 56 * 1024 * 1024

TOKEN_TILE = 512
Q_BLOCK = 128
K_WINDOW = Q_BLOCK + 2 * HALF_WINDOW
SUPER_BLOCK = Q_BLOCK * DILATIONS[-1]
ATTN_UNROLL = 8
GLA_BLOCK = 512
FF_CHUNK = 1408


def _const_spec(shape):
    return pl.BlockSpec(shape, lambda *_: (0,) * len(shape), pipeline_mode=pl.Buffered(1))


def _inproj_kernel(x_ref, g_ref, w_ref, cos_ref, sina_ref, sinb_ref, up_ref, gb_ref,
                   qa_ref, ka_ref, va_ref, qg_ref, kg_ref, vg_ref, og_ref, laf_ref, lab_ref):
    x = x_ref[...]
    ms = jnp.mean(x * x, axis=-1, keepdims=True)
    hn = (x * lax.rsqrt(ms + EPS) * g_ref[...]).astype(BF16)

    def proj(col0, width):
        return jnp.dot(hn, w_ref[:, col0:col0 + width], preferred_element_type=F32)

    cos = cos_ref[...]
    sina = sina_ref[...]
    sinb = sinb_ref[...]

    def rope(t):
        out = []
        for j in range(t.shape[1] // LANES):
            s = t[:, j * LANES:(j + 1) * LANES]
            out.append(s * cos + pltpu.roll(s, HEAD_DIM // 2, 1) * sina
                       + pltpu.roll(s, LANES - HEAD_DIM // 2, 1) * sinb)
        return jnp.concatenate(out, axis=1)

    qa_ref[...] = rope(proj(0, ATTN_WIDTH)) * (HEAD_DIM ** -0.5 * LOG2_E)
    ka_ref[...] = rope(proj(ATTN_WIDTH, ATTN_WIDTH))
    va_ref[...] = proj(2 * ATTN_WIDTH, ATTN_WIDTH)
    c = 3 * ATTN_WIDTH
    qg_ref[...] = proj(c, GLA_KW) * (GLA_DK ** -0.5)
    kg_ref[...] = proj(c + GLA_KW, GLA_KW)
    vg_ref[...] = proj(c + 2 * GLA_KW, GLA_VW).astype(BF16)
    og_ref[...] = proj(c + 2 * GLA_KW + GLA_VW, GLA_VW).astype(BF16)
    r = proj(GATE_COL0, LANES).astype(BF16)
    z = jnp.dot(r, up_ref[...], preferred_element_type=F32) + gb_ref[...]
    la = (jnp.minimum(z, 0.0) - jnp.log(1.0 + jnp.exp(-jnp.abs(z)))) * (1.0 / GLA_TAU)
    laf_ref[...] = la[:, :GLA_KW]
    lab_ref[...] = la[:, GLA_KW:]


def _inproj(x2, seq, g, w, cos, sina, sinb, up, gb):
    t = x2.shape[0]
    tm = TOKEN_TILE
    n_pos = seq // tm
    row = lambda i: (i, 0)
    pos = lambda i: (i % n_pos, 0)
    out_shapes = [
        jax.ShapeDtypeStruct((t, ATTN_WIDTH), F32),
        jax.ShapeDtypeStruct((t, ATTN_WIDTH), F32),
        jax.ShapeDtypeStruct((t, ATTN_WIDTH), F32),
        jax.ShapeDtypeStruct((t, GLA_KW), F32),
        jax.ShapeDtypeStruct((t, GLA_KW), F32),
        jax.ShapeDtypeStruct((t, GLA_VW), BF16),
        jax.ShapeDtypeStruct((t, GLA_VW), BF16),
        jax.ShapeDtypeStruct((t, GLA_KW), F32),
        jax.ShapeDtypeStruct((t, GLA_KW), F32),
    ]
    return pl.pallas_call(
        _inproj_kernel,
        grid=(t // tm,),
        in_specs=[
            pl.BlockSpec((tm, D_MODEL), row),
            _const_spec((1, D_MODEL)),
            _const_spec((D_MODEL, IN_COLS_PAD)),
            pl.BlockSpec((tm, LANES), pos),
            pl.BlockSpec((tm, LANES), pos),
            pl.BlockSpec((tm, LANES), pos),
            _const_spec((LANES, 2 * GLA_KW)),
            _const_spec((1, 2 * GLA_KW)),
        ],
        out_specs=[pl.BlockSpec((tm, s.shape[1]), row) for s in out_shapes],
        out_shape=out_shapes,
        compiler_params=pltpu.CompilerParams(
            dimension_semantics=("parallel",), vmem_limit_bytes=VMEM_LIMIT),
        name="inproj",
    )(x2, g, w, cos, sina, sinb, up, gb)


def _attn_kernel(q_ref, k_ref, v_ref, o_ref, o4_ref, lse4_ref, o16_ref, lse16_ref, bias_ref,
                 *, seq):
    lane = lax.broadcasted_iota(jnp.int32, (Q_BLOCK, LANES), 1)
    head0 = lane < HEAD_DIM
    ones = jnp.ones((K_WINDOW, LANES), BF16)

    col_minus_row = (lax.broadcasted_iota(jnp.int32, (Q_BLOCK, K_WINDOW), 1)
                     - lax.broadcasted_iota(jnp.int32, (Q_BLOCK, K_WINDOW), 0))
    for e in range(bias_ref.shape[0]):
        bias_ref[e] = jnp.where(jnp.abs(col_minus_row - HALF_WINDOW * e) <= HALF_WINDOW,
                                0.0, MASK_VALUE)

    def pattern(d, cls, sub0):
        n_sub = seq // d
        win0 = jnp.clip(sub0 - HALF_WINDOW, 0, n_sub - K_WINDOW)
        if d > 1:
            q_rows = pl.ds(cls + d * sub0, Q_BLOCK, stride=d)
            k_rows = pl.ds(cls + d * win0, K_WINDOW, stride=d)
        else:
            q_rows = pl.ds(pl.multiple_of(sub0, Q_BLOCK), Q_BLOCK)
            k_rows = pl.ds(pl.multiple_of(win0, HALF_WINDOW), K_WINDOW)
        q = q_ref[q_rows, :]
        k = k_ref[k_rows, :].astype(BF16)
        v = jnp.concatenate([v_ref[k_rows, :].astype(BF16), ones], axis=1)
        q2 = jnp.concatenate([jnp.where(head0, q, 0.0), jnp.where(head0, 0.0, q)],
                             axis=0).astype(BF16)
        s = lax.dot_general(q2, k, (((1,), (1,)), ((), ())), preferred_element_type=F32)
        bias = bias_ref[(sub0 - win0) // HALF_WINDOW]
        ps, ms = [], []
        for h in range(2):
            sh = s[h * Q_BLOCK:(h + 1) * Q_BLOCK] + bias
            m = jnp.max(sh, axis=-1, keepdims=True)
            ms.append(jnp.broadcast_to(m, (Q_BLOCK, LANES)))
            ps.append(jnp.exp2(sh - m).astype(BF16))
        pv = jnp.dot(jnp.concatenate(ps, axis=0), v, preferred_element_type=F32)
        l = jnp.where(head0, pv[:Q_BLOCK, LANES:], pv[Q_BLOCK:, LANES:])
        out = jnp.where(head0, pv[:Q_BLOCK, :LANES], pv[Q_BLOCK:, :LANES]) / l
        return out, jnp.where(head0, ms[0], ms[1]) + jnp.log2(l)

    def super_block(sb, carry):
        base = pl.multiple_of(sb * SUPER_BLOCK, SUPER_BLOCK)

        for d, o_d, lse_d in ((16, o16_ref, lse16_ref), (4, o4_ref, lse4_ref)):
            blocks_per_class = SUPER_BLOCK // d // Q_BLOCK

            def park(i, c, d=d, bpc=blocks_per_class, o_d=o_d, lse_d=lse_d):
                cls = i // bpc
                blk = (i % bpc) * Q_BLOCK
                out, lse = pattern(d, cls, sb * (SUPER_BLOCK // d) + blk)
                rows = pl.ds(cls + d * blk, Q_BLOCK, stride=d)
                o_d[rows, :] = out
                lse_d[rows, :] = lse
                return c

            lax.fori_loop(0, SUPER_BLOCK // Q_BLOCK, park, 0, unroll=ATTN_UNROLL)

        def merge(i, c):
            blk = pl.multiple_of(i * Q_BLOCK, Q_BLOCK)
            out1, lse1 = pattern(1, 0, base + blk)
            rows = pl.ds(blk, Q_BLOCK)
            lse4 = lse4_ref[rows, :]
            lse16 = lse16_ref[rows, :]
            top = jnp.maximum(jnp.maximum(lse1, lse4), lse16)
            w1 = jnp.exp2(lse1 - top)
            w4 = jnp.exp2(lse4 - top)
            w16 = jnp.exp2(lse16 - top)
            mix = (w1 * out1 + w4 * o4_ref[rows, :] + w16 * o16_ref[rows, :]) / (w1 + w4 + w16)
            o_ref[pl.ds(base + blk, Q_BLOCK), :] = mix.astype(o_ref.dtype)
            return c

        lax.fori_loop(0, SUPER_BLOCK // Q_BLOCK, merge, 0, unroll=ATTN_UNROLL)
        return carry

    lax.fori_loop(0, seq // SUPER_BLOCK, super_block, 0)


def _attention(q, k, v):
    b, seq, _ = q.shape
    spec = pl.BlockSpec((None, seq, LANES), lambda i, h: (i, 0, h))
    stat = pltpu.VMEM((SUPER_BLOCK, LANES), F32)
    return pl.pallas_call(
        functools.partial(_attn_kernel, seq=seq),
        grid=(b, ATTN_WIDTH // LANES),
        in_specs=[spec, spec, spec],
        out_specs=spec,
        out_shape=jax.ShapeDtypeStruct((b, seq, ATTN_WIDTH), BF16),
        scratch_shapes=[stat, stat, stat, stat, pltpu.VMEM((3, Q_BLOCK, K_WINDOW), F32)],
        compiler_params=pltpu.CompilerParams(
            dimension_semantics=("parallel", "parallel"), vmem_limit_bytes=VMEM_LIMIT),
        name="dilated_attn",
    )(q, k, v)


def _gla_kernel(qf_ref, kf_ref, vf_ref, laf_ref, qb_ref, kb_ref, vb_ref, lab_ref,
                of_ref, ob_ref, stf_ref, stb_ref):
    c = GLA_CHUNK

    @pl.when(pl.program_id(1) == 0)
    def _():
        stf_ref[...] = jnp.zeros_like(stf_ref)
        stb_ref[...] = jnp.zeros_like(stb_ref)

    row = lax.broadcasted_iota(jnp.int32, (c, 2 * c), 0)
    col = lax.broadcasted_iota(jnp.int32, (c, 2 * c), 1) % c
    tri_f = jnp.where(col <= row, 1.0, 0.0).astype(BF16)
    tri_b = jnp.where(col >= row, 1.0, 0.0).astype(BF16)
    row4 = lax.broadcasted_iota(jnp.int32, (c, GLA_HEADS * c), 0)
    col4 = lax.broadcasted_iota(jnp.int32, (c, GLA_HEADS * c), 1) % c
    keep_f = col4 <= row4
    keep_b = col4 > row4
    klane = lax.broadcasted_iota(jnp.int32, (c, GLA_KW), 1) // GLA_DK
    slane = lax.broadcasted_iota(jnp.int32, (GLA_DV, GLA_KW), 1) // GLA_DK
    zeros_v = jnp.zeros((c, GLA_DV), BF16)
    nt = (((1,), (1,)), ((), ()))
    tn = (((0,), (0,)), ((), ()))

    def by_head(t, lane_head):
        return jnp.concatenate(
            [jnp.where(lane_head == h, t, 0.0) for h in range(GLA_HEADS)], axis=0).astype(BF16)

    n_chunks = GLA_BLOCK // c
    streams = (
        (qf_ref, kf_ref, vf_ref, laf_ref, of_ref, stf_ref, tri_f, keep_f, c - 1,
         [j * c for j in range(n_chunks)]),
        (qb_ref, kb_ref, vb_ref, lab_ref, ob_ref, stb_ref, tri_b, keep_b, 0,
         [(n_chunks - 1 - j) * c for j in range(n_chunks)]),
    )
    jobs = [(s, streams[s][9][j]) for j in range(n_chunks) for s in range(2)]

    def rows(ref, r0):
        return ref[r0:r0 + c, :]

    cums = []
    for s, r0 in jobs:
        la = rows(streams[s][3], r0)
        la_hi = la.astype(BF16)
        la_lo = (la - la_hi.astype(F32)).astype(BF16)
        cums.append(jnp.dot(streams[s][6], jnp.concatenate([la_hi, la_lo], axis=0),
                            preferred_element_type=F32))

    q_es, decays, attns, kv_ts = [], [], [], []
    for (s, r0), cum in zip(jobs, cums):
        q_ref, k_ref, v_ref = streams[s][0:3]
        last_row = streams[s][8]
        cum_last = cum[last_row:last_row + 1, :]
        k = rows(k_ref, r0)
        q_e = (rows(q_ref, r0) * jnp.exp(cum)).astype(BF16)
        k_e = k * jnp.exp(-cum)
        k_end = k * jnp.exp(cum_last - cum)
        q_es.append(q_e)
        decays.append(jnp.exp(cum_last))
        attns.append(lax.dot_general(q_e, by_head(k_e, klane), nt, preferred_element_type=F32))
        v = rows(v_ref, r0)
        v_heads = jnp.concatenate(
            [v[:, h * GLA_DV:(h + 1) * GLA_DV] for h in range(GLA_HEADS)], axis=0)
        kv_ts.append(lax.dot_general(v_heads, by_head(k_end, klane), tn,
                                     preferred_element_type=F32))

    o_intras = []
    for (s, r0), attn in zip(jobs, attns):
        v = rows(streams[s][2], r0)
        v_h = [v[:, h * GLA_DV:(h + 1) * GLA_DV] for h in range(GLA_HEADS)]
        v_diag = jnp.concatenate(
            [jnp.concatenate([v_h[h] if h == g else zeros_v for h in range(GLA_HEADS)], axis=0)
             for g in range(GLA_HEADS)], axis=1)
        attn = jnp.where(streams[s][7], attn, 0.0).astype(BF16)
        o_intras.append(jnp.dot(attn, v_diag, preferred_element_type=F32))

    st = [streams[0][5][...], streams[1][5][...]]
    for (s, r0), q_e, decay, kv_t, o_intra in zip(jobs, q_es, decays, kv_ts, o_intras):
        o_state = lax.dot_general(q_e, by_head(st[s], slane), nt, preferred_element_type=F32)
        streams[s][4][r0:r0 + c, :] = o_intra + o_state
        st[s] = st[s] * decay + kv_t
    streams[0][5][...] = st[0]
    streams[1][5][...] = st[1]


def _gla(q, k, v, la_f, la_b):
    b, seq, _ = q.shape
    n = seq // GLA_BLOCK
    fwd = lambda w: pl.BlockSpec((None, GLA_BLOCK, w), lambda i, j: (i, j, 0))
    bwd = lambda w: pl.BlockSpec((None, GLA_BLOCK, w), lambda i, j: (i, n - 1 - j, 0))
    out = jax.ShapeDtypeStruct((b, seq, GLA_VW), F32)
    state = pltpu.VMEM((GLA_DV, GLA_KW), F32)
    return pl.pallas_call(
        _gla_kernel,
        grid=(b, n),
        in_specs=[fwd(GLA_KW), fwd(GLA_KW), fwd(GLA_VW), fwd(GLA_KW),
                  bwd(GLA_KW), bwd(GLA_KW), bwd(GLA_VW), bwd(GLA_KW)],
        out_specs=[fwd(GLA_VW), bwd(GLA_VW)],
        out_shape=[out, out],
        scratch_shapes=[state, state],
        compiler_params=pltpu.CompilerParams(
            dimension_semantics=("parallel", "arbitrary"), vmem_limit_bytes=VMEM_LIMIT),
        name="gla",
    )(q, k, v, la_f, q, k, v, la_b)


def _out_ffn_kernel(x_ref, attn_ref, of_ref, ob_ref, og_ref, gn_ref, wo_ref, g2_ref,
                    wg_ref, wu_ref, wd_ref, gf_ref, y_ref):
    def rms(t, g):
        return t * lax.rsqrt(jnp.mean(t * t, axis=-1, keepdims=True) + EPS) * g

    o = of_ref[...] + ob_ref[...]
    og = og_ref[...].astype(F32)
    gate = og * (1.0 / (1.0 + jnp.exp(-og)))
    heads = []
    for h in range(GLA_HEADS):
        sl = slice(h * GLA_DV, (h + 1) * GLA_DV)
        heads.append(rms(o[:, sl], gn_ref[:, sl]) * gate[:, sl])
    mixed = jnp.concatenate([attn_ref[...]] + [t.astype(BF16) for t in heads], axis=1)
    x1 = x_ref[...] + jnp.dot(mixed, wo_ref[...], preferred_element_type=F32)
    h2 = rms(x1, g2_ref[...]).astype(BF16)
    ff = jnp.zeros_like(x1)
    for j in range(D_FF // FF_CHUNK):
        cs = slice(j * FF_CHUNK, (j + 1) * FF_CHUNK)
        a = jnp.dot(h2, wg_ref[:, cs], preferred_element_type=F32)
        u = jnp.dot(h2, wu_ref[:, cs], preferred_element_type=F32)
        act = (a * (1.0 / (1.0 + jnp.exp(-a))) * u).astype(BF16)
        ff = ff + jnp.dot(act, wd_ref[cs, :], preferred_element_type=F32)
    y_ref[...] = rms(x1 + ff, gf_ref[...])


def _out_ffn(x2, attn, o_f, o_b, og, gn, wo, g2, wg, wu, wd, gf):
    t = x2.shape[0]
    tm = TOKEN_TILE
    row = lambda i: (i, 0)
    return pl.pallas_call(
        _out_ffn_kernel,
        grid=(t // tm,),
        in_specs=[
            pl.BlockSpec((tm, D_MODEL), row),
            pl.BlockSpec((tm, ATTN_WIDTH), row),
            pl.BlockSpec((tm, GLA_VW), row),
            pl.BlockSpec((tm, GLA_VW), row),
            pl.BlockSpec((tm, GLA_VW), row),
            _const_spec((1, GLA_VW)),
            _const_spec((D_MODEL, D_MODEL)),
            _const_spec((1, D_MODEL)),
            _const_spec((D_MODEL, D_FF)),
            _const_spec((D_MODEL, D_FF)),
            _const_spec((D_FF, D_MODEL)),
            _const_spec((1, D_MODEL)),
        ],
        out_specs=pl.BlockSpec((tm, D_MODEL), row),
        out_shape=jax.ShapeDtypeStruct((t, D_MODEL), F32),
        compiler_params=pltpu.CompilerParams(
            dimension_semantics=("parallel",), vmem_limit_bytes=VMEM_LIMIT),
        name="out_ffn",
    )(x2, attn, o_f, o_b, og, gn, wo, g2, wg, wu, wd, gf)


def _rope_tables(seq):
    inv_freq = ROPE_THETA ** (-jnp.arange(0, HEAD_DIM, 2, dtype=F32) / HEAD_DIM)
    ang = jnp.arange(seq, dtype=F32)[:, None] * inv_freq[None, :]
    cos = jnp.tile(jnp.cos(ang), (1, LANES // (HEAD_DIM // 2)))
    sin = jnp.tile(jnp.sin(ang), (1, LANES // (HEAD_DIM // 2)))
    second_half = (jnp.arange(LANES) % HEAD_DIM) >= HEAD_DIM // 2
    sina = jnp.where(second_half[None, :], sin, 0.0)
    sinb = jnp.where(second_half[None, :], 0.0, -sin)
    return cos, sina, sinb


def _trunk(x, p):
    b, seq, _ = x.shape
    assert seq % SUPER_BLOCK == 0 and seq // DILATIONS[-1] >= K_WINDOW
    assert seq % GLA_BLOCK == 0 and (b * seq) % TOKEN_TILE == 0 and seq % TOKEN_TILE == 0
    x2 = x.reshape(b * seq, D_MODEL)
    cos, sina, sinb = _rope_tables(seq)
    qa, ka, va, qg, kg, vg, og, la_f, la_b = _inproj(
        x2, seq, p["g1"], p["w_in"], cos, sina, sinb, p["gate_up"], p["gate_bias"])
    r3 = lambda a: a.reshape(b, seq, a.shape[-1])
    attn = _attention(r3(qa), r3(ka), r3(va))
    o_f, o_b = _gla(r3(qg), r3(kg), r3(vg), r3(la_f), r3(la_b))
    r2 = lambda a: a.reshape(b * seq, a.shape[-1])
    y = _out_ffn(x2, r2(attn), r2(o_f), r2(o_b), og, p["gn"], p["w_out"], p["g2"],
                 p["w_gate"], p["w_up"], p["w_down"], p["gf"])
    return y.reshape(b, seq, D_MODEL)


def kernel(x_prompt, x_sample, norm1_g, w_in, gate_up_fwd, gate_bias_fwd, gate_up_bwd, gate_bias_bwd,
           gla_norm_g, w_out, norm2_g, w_gate, w_up, w_down, final_norm_g):
    assert norm1_g.shape[0] == 1, "single layer"
    w_in_p = jnp.pad(w_in[0], ((0, 0), (0, IN_COLS_PAD - IN_COLS))).astype(BF16)
    up = jnp.zeros((LANES, 2 * GLA_KW), F32)
    up = up.at[:GLA_GATE_RANK, :GLA_KW].set(gate_up_fwd[0])
    up = up.at[GLA_GATE_RANK:2 * GLA_GATE_RANK, GLA_KW:].set(gate_up_bwd[0])
    p = {
        "g1": norm1_g[0][None, :],
        "w_in": w_in_p,
        "gate_up": up.astype(BF16),
        "gate_bias": jnp.concatenate([gate_bias_fwd[0], gate_bias_bwd[0]])[None, :],
        "gn": gla_norm_g[0][None, :],
        "w_out": w_out[0].astype(BF16),
        "g2": norm2_g[0][None, :],
        "w_gate": w_gate[0].astype(BF16),
        "w_up": w_up[0].astype(BF16),
        "w_down": w_down[0].astype(BF16),
        "gf": final_norm_g[None, :],
    }
    return _trunk(x_prompt, p), _trunk(x_sample, p)
```

```python
import functools

import jax
import jax.numpy as jnp
from jax import lax
from jax.experimental import pallas as pl
from jax.experimental.pallas import tpu as pltpu

F32 = jnp.float32
BF16 = jnp.bfloat16

D_MODEL = 1024
HEAD_DIM = 64
ATTN_HEADS = 8
ATTN_WIDTH = ATTN_HEADS * HEAD_DIM
DILATIONS = (1, 4, 16)
HALF_WINDOW = 64
ROPE_THETA = 10000.0
GLA_HEADS = 4
GLA_DK = 64
GLA_DV = 128
GLA_KW = GLA_HEADS * GLA_DK
GLA_VW = GLA_HEADS * GLA_DV
GLA_GATE_RANK = 16
GLA_TAU = 16.0
GLA_CHUNK = 64
D_FF = 2816
EPS = 1e-6
MASK_VALUE = -1e30
LOG2_E = 1.4426950408889634

LANES = 128
IN_COLS = 3 * ATTN_WIDTH + 2 * GLA_KW + 2 * GLA_VW + 2 * GLA_GATE_RANK
IN_COLS_PAD = 3200
GATE_COL0 = 3 * ATTN_WIDTH + 2 * GLA_KW + 2 * GLA_VW
VMEM_LIMIT = 56 * 1024 * 1024

TOKEN_TILE = 512
Q_BLOCK = 128
K_WINDOW = Q_BLOCK + 2 * HALF_WINDOW
SUPER_BLOCK = Q_BLOCK * DILATIONS[-1]
ATTN_UNROLL = 8
GLA_BLOCK = 512
FF_CHUNKS = ((0, 1536), (1536, D_FF))
ROW_SPLIT = 2


def _const_spec(shape):
    return pl.BlockSpec(shape, lambda *_: (0,) * len(shape), pipeline_mode=pl.Buffered(1))


def _inproj_kernel(x_ref, g_ref, w_ref, cos_ref, sina_ref, sinb_ref, up_ref, gb_ref,
                   qa_ref, ka_ref, va_ref, qg_ref, kg_ref, vg_ref, og_ref, laf_ref, lab_ref):
    groups = [slice(i * (TOKEN_TILE // ROW_SPLIT), (i + 1) * (TOKEN_TILE // ROW_SPLIT))
              for i in range(ROW_SPLIT)]
    hns = []
    for rs in groups:
        x = x_ref[rs, :]
        ms = jnp.mean(x * x, axis=-1, keepdims=True)
        hns.append((x * lax.rsqrt(ms + EPS) * g_ref[...]).astype(BF16))

    def proj(hn, col0, width):
        return jnp.dot(hn, w_ref[:, col0:col0 + width], preferred_element_type=F32)

    def rope(t, rs):
        cos = cos_ref[rs, :]
        sina = sina_ref[rs, :]
        sinb = sinb_ref[rs, :]
        out = []
        for j in range(t.shape[1] // LANES):
            s = t[:, j * LANES:(j + 1) * LANES]
            out.append(s * cos + pltpu.roll(s, HEAD_DIM // 2, 1) * sina
                       + pltpu.roll(s, LANES - HEAD_DIM // 2, 1) * sinb)
        return jnp.concatenate(out, axis=1)

    c = 3 * ATTN_WIDTH
    for rs, hn in zip(groups, hns):
        qa_ref[rs, :] = rope(proj(hn, 0, ATTN_WIDTH), rs) * (HEAD_DIM ** -0.5 * LOG2_E)
    for rs, hn in zip(groups, hns):
        ka_ref[rs, :] = rope(proj(hn, ATTN_WIDTH, ATTN_WIDTH), rs)
    for rs, hn in zip(groups, hns):
        va_ref[rs, :] = proj(hn, 2 * ATTN_WIDTH, ATTN_WIDTH)
    for rs, hn in zip(groups, hns):
        qg_ref[rs, :] = proj(hn, c, GLA_KW) * (GLA_DK ** -0.5)
    for rs, hn in zip(groups, hns):
        kg_ref[rs, :] = proj(hn, c + GLA_KW, GLA_KW)
    for rs, hn in zip(groups, hns):
        vg_ref[rs, :] = proj(hn, c + 2 * GLA_KW, GLA_VW).astype(BF16)
    for rs, hn in zip(groups, hns):
        og_ref[rs, :] = proj(hn, c + 2 * GLA_KW + GLA_VW, GLA_VW).astype(BF16)
    for rs, hn in zip(groups, hns):
        r = proj(hn, GATE_COL0, LANES).astype(BF16)
        z = jnp.dot(r, up_ref[...], preferred_element_type=F32) + gb_ref[...]
        la = (jnp.minimum(z, 0.0) - jnp.log(1.0 + jnp.exp(-jnp.abs(z)))) * (1.0 / GLA_TAU)
        laf_ref[rs, :] = la[:, :GLA_KW]
        lab_ref[rs, :] = la[:, GLA_KW:]


def _inproj(x2, seq, g, w, cos, sina, sinb, up, gb):
    t = x2.shape[0]
    tm = TOKEN_TILE
    n_pos = seq // tm
    row = lambda i: (i, 0)
    pos = lambda i: (i % n_pos, 0)
    out_shapes = [
        jax.ShapeDtypeStruct((t, ATTN_WIDTH), F32),
        jax.ShapeDtypeStruct((t, ATTN_WIDTH), F32),
        jax.ShapeDtypeStruct((t, ATTN_WIDTH), F32),
        jax.ShapeDtypeStruct((t, GLA_KW), F32),
        jax.ShapeDtypeStruct((t, GLA_KW), F32),
        jax.ShapeDtypeStruct((t, GLA_VW), BF16),
        jax.ShapeDtypeStruct((t, GLA_VW), BF16),
        jax.ShapeDtypeStruct((t, GLA_KW), F32),
        jax.ShapeDtypeStruct((t, GLA_KW), F32),
    ]
    return pl.pallas_call(
        _inproj_kernel,
        grid=(t // tm,),
        in_specs=[
            pl.BlockSpec((tm, D_MODEL), row),
            _const_spec((1, D_MODEL)),
            _const_spec((D_MODEL, IN_COLS_PAD)),
            pl.BlockSpec((tm, LANES), pos),
            pl.BlockSpec((tm, LANES), pos),
            pl.BlockSpec((tm, LANES), pos),
            _const_spec((LANES, 2 * GLA_KW)),
            _const_spec((1, 2 * GLA_KW)),
        ],
        out_specs=[pl.BlockSpec((tm, s.shape[1]), row) for s in out_shapes],
        out_shape=out_shapes,
        compiler_params=pltpu.CompilerParams(
            dimension_semantics=("parallel",), vmem_limit_bytes=VMEM_LIMIT),
        name="inproj",
    )(x2, g, w, cos, sina, sinb, up, gb)


def _attn_kernel(q_ref, k_ref, v_ref, o_ref, o4_ref, lse4_ref, o16_ref, lse16_ref, bias_ref,
                 *, seq):
    lane = lax.broadcasted_iota(jnp.int32, (Q_BLOCK, LANES), 1)
    head0 = lane < HEAD_DIM
    ones = jnp.ones((K_WINDOW, LANES), BF16)

    col_minus_row = (lax.broadcasted_iota(jnp.int32, (Q_BLOCK, K_WINDOW), 1)
                     - lax.broadcasted_iota(jnp.int32, (Q_BLOCK, K_WINDOW), 0))
    for e in range(bias_ref.shape[0]):
        bias_ref[e] = jnp.where(jnp.abs(col_minus_row - HALF_WINDOW * e) <= HALF_WINDOW,
                                0.0, MASK_VALUE)

    def pattern(d, cls, sub0):
        n_sub = seq // d
        win0 = jnp.clip(sub0 - HALF_WINDOW, 0, n_sub - K_WINDOW)
        if d > 1:
            q_rows = pl.ds(cls + d * sub0, Q_BLOCK, stride=d)
            k_rows = pl.ds(cls + d * win0, K_WINDOW, stride=d)
        else:
            q_rows = pl.ds(pl.multiple_of(sub0, Q_BLOCK), Q_BLOCK)
            k_rows = pl.ds(pl.multiple_of(win0, HALF_WINDOW), K_WINDOW)
        q = q_ref[q_rows, :]
        k = k_ref[k_rows, :].astype(BF16)
        v = jnp.concatenate([v_ref[k_rows, :].astype(BF16), ones], axis=1)
        q2 = jnp.concatenate([jnp.where(head0, q, 0.0), jnp.where(head0, 0.0, q)],
                             axis=0).astype(BF16)
        s = lax.dot_general(q2, k, (((1,), (1,)), ((), ())), preferred_element_type=F32)
        bias = bias_ref[(sub0 - win0) // HALF_WINDOW]
        ps, ms = [], []
        for h in range(2):
            sh = s[h * Q_BLOCK:(h + 1) * Q_BLOCK] + bias
            m = jnp.max(sh, axis=-1, keepdims=True)
            ms.append(jnp.broadcast_to(m, (Q_BLOCK, LANES)))
            ps.append(jnp.exp2(sh - m).astype(BF16))
        pv = jnp.dot(jnp.concatenate(ps, axis=0), v, preferred_element_type=F32)
        l = jnp.where(head0, pv[:Q_BLOCK, LANES:], pv[Q_BLOCK:, LANES:])
        out = jnp.where(head0, pv[:Q_BLOCK, :LANES], pv[Q_BLOCK:, :LANES]) / l
        return out, jnp.where(head0, ms[0], ms[1]) + jnp.log2(l)

    def super_block(sb, carry):
        base = pl.multiple_of(sb * SUPER_BLOCK, SUPER_BLOCK)

        for d, o_d, lse_d in ((16, o16_ref, lse16_ref), (4, o4_ref, lse4_ref)):
            blocks_per_class = SUPER_BLOCK // d // Q_BLOCK

            def park(i, c, d=d, bpc=blocks_per_class, o_d=o_d, lse_d=lse_d):
                cls = i // bpc
                blk = (i % bpc) * Q_BLOCK
                out, lse = pattern(d, cls, sb * (SUPER_BLOCK // d) + blk)
                rows = pl.ds(cls + d * blk, Q_BLOCK, stride=d)
                o_d[rows, :] = out
                lse_d[rows, :] = lse
                return c

            lax.fori_loop(0, SUPER_BLOCK // Q_BLOCK, park, 0, unroll=ATTN_UNROLL)

        def merge(i, c):
            blk = pl.multiple_of(i * Q_BLOCK, Q_BLOCK)
            out1, lse1 = pattern(1, 0, base + blk)
            rows = pl.ds(blk, Q_BLOCK)
            lse4 = lse4_ref[rows, :]
            lse16 = lse16_ref[rows, :]
            top = jnp.maximum(jnp.maximum(lse1, lse4), lse16)
            w1 = jnp.exp2(lse1 - top)
            w4 = jnp.exp2(lse4 - top)
            w16 = jnp.exp2(lse16 - top)
            mix = (w1 * out1 + w4 * o4_ref[rows, :] + w16 * o16_ref[rows, :]) / (w1 + w4 + w16)
            o_ref[pl.ds(base + blk, Q_BLOCK), :] = mix.astype(o_ref.dtype)
            return c

        lax.fori_loop(0, SUPER_BLOCK // Q_BLOCK, merge, 0, unroll=ATTN_UNROLL)
        return carry

    lax.fori_loop(0, seq // SUPER_BLOCK, super_block, 0)


def _attention(q, k, v):
    b, seq, _ = q.shape
    spec = pl.BlockSpec((None, seq, LANES), lambda i, h: (i, 0, h))
    stat = pltpu.VMEM((SUPER_BLOCK, LANES), F32)
    return pl.pallas_call(
        functools.partial(_attn_kernel, seq=seq),
        grid=(b, ATTN_WIDTH // LANES),
        in_specs=[spec, spec, spec],
        out_specs=spec,
        out_shape=jax.ShapeDtypeStruct((b, seq, ATTN_WIDTH), BF16),
        scratch_shapes=[stat, stat, stat, stat, pltpu.VMEM((3, Q_BLOCK, K_WINDOW), F32)],
        compiler_params=pltpu.CompilerParams(
            dimension_semantics=("parallel", "parallel"), vmem_limit_bytes=VMEM_LIMIT),
        name="dilated_attn",
    )(q, k, v)


def _gla_kernel(qf_ref, kf_ref, vf_ref, laf_ref, qb_ref, kb_ref, vb_ref, lab_ref,
                of_ref, ob_ref, stf_ref, stb_ref):
    c = GLA_CHUNK

    @pl.when(pl.program_id(1) == 0)
    def _():
        stf_ref[...] = jnp.zeros_like(stf_ref)
        stb_ref[...] = jnp.zeros_like(stb_ref)

    row = lax.broadcasted_iota(jnp.int32, (c, 2 * c), 0)
    col = lax.broadcasted_iota(jnp.int32, (c, 2 * c), 1) % c
    tri_f = jnp.where(col <= row, 1.0, 0.0).astype(BF16)
    tri_b = jnp.where(col >= row, 1.0, 0.0).astype(BF16)
    row4 = lax.broadcasted_iota(jnp.int32, (c, GLA_HEADS * c), 0)
    col4 = lax.broadcasted_iota(jnp.int32, (c, GLA_HEADS * c), 1) % c
    keep_f = col4 <= row4
    keep_b = col4 > row4
    klane = lax.broadcasted_iota(jnp.int32, (c, GLA_KW), 1) // GLA_DK
    zeros_v = jnp.zeros((c, GLA_DV), BF16)
    nt = (((1,), (1,)), ((), ()))
    tn = (((0,), (0,)), ((), ()))

    def by_head(t, lane_head):
        return jnp.concatenate(
            [jnp.where(lane_head == h, t, 0.0) for h in range(GLA_HEADS)], axis=0).astype(BF16)

    n_chunks = GLA_BLOCK // c
    streams = (
        (qf_ref, kf_ref, vf_ref, laf_ref, of_ref, stf_ref, tri_f, keep_f, c - 1,
         [j * c for j in range(n_chunks)]),
        (qb_ref, kb_ref, vb_ref, lab_ref, ob_ref, stb_ref, tri_b, keep_b, 0,
         [(n_chunks - 1 - j) * c for j in range(n_chunks)]),
    )
    jobs = [(s, streams[s][9][j]) for j in range(n_chunks) for s in range(2)]

    def rows(ref, r0):
        return ref[r0:r0 + c, :]

    cums = []
    for s, r0 in jobs:
        la = rows(streams[s][3], r0)
        la_hi = la.astype(BF16)
        la_lo = (la - la_hi.astype(F32)).astype(BF16)
        cums.append(jnp.dot(streams[s][6], jnp.concatenate([la_hi, la_lo], axis=0),
                            preferred_element_type=F32))

    q_es, decays, attns, kv_ts = [], [], [], []
    for (s, r0), cum in zip(jobs, cums):
        q_ref, k_ref, v_ref = streams[s][0:3]
        last_row = streams[s][8]
        cum_last = cum[last_row:last_row + 1, :]
        k = rows(k_ref, r0)
        q_e_f32 = rows(q_ref, r0) * jnp.exp(cum)
        q_e = q_e_f32.astype(BF16)
        k_e = k * jnp.exp(-cum)
        k_end = k * jnp.exp(cum_last - cum)
        q_es.append(by_head(q_e_f32, klane))
        decays.append(jnp.exp(cum_last))
        attns.append(lax.dot_general(q_e, by_head(k_e, klane), nt, preferred_element_type=F32))
        v = rows(v_ref, r0)
        v_heads = jnp.concatenate(
            [v[:, h * GLA_DV:(h + 1) * GLA_DV] for h in range(GLA_HEADS)], axis=0)
        kv_ts.append(lax.dot_general(v_heads, by_head(k_end, klane), tn,
                                     preferred_element_type=F32))

    o_intras = []
    for (s, r0), attn in zip(jobs, attns):
        v = rows(streams[s][2], r0)
        v_h = [v[:, h * GLA_DV:(h + 1) * GLA_DV] for h in range(GLA_HEADS)]
        v_diag = jnp.concatenate(
            [jnp.concatenate([v_h[h] if h == g else zeros_v for h in range(GLA_HEADS)], axis=0)
             for g in range(GLA_HEADS)], axis=1)
        attn = jnp.where(streams[s][7], attn, 0.0).astype(BF16)
        o_intras.append(jnp.dot(attn, v_diag, preferred_element_type=F32))

    st = [streams[0][5][...], streams[1][5][...]]
    for (s, r0), q_heads, decay, kv_t, o_intra in zip(jobs, q_es, decays, kv_ts, o_intras):
        o_state = lax.dot_general(q_heads, st[s].astype(BF16), nt, preferred_element_type=F32)
        streams[s][4][r0:r0 + c, :] = o_intra + jnp.concatenate(
            [o_state[h * c:(h + 1) * c] for h in range(GLA_HEADS)], axis=1)
        st[s] = st[s] * decay + kv_t
    streams[0][5][...] = st[0]
    streams[1][5][...] = st[1]


def _gla(q, k, v, la_f, la_b):
    b, seq, _ = q.shape
    n = seq // GLA_BLOCK
    fwd = lambda w: pl.BlockSpec((None, GLA_BLOCK, w), lambda i, j: (i, j, 0))
    bwd = lambda w: pl.BlockSpec((None, GLA_BLOCK, w), lambda i, j: (i, n - 1 - j, 0))
    out = jax.ShapeDtypeStruct((b, seq, GLA_VW), F32)
    state = pltpu.VMEM((GLA_DV, GLA_KW), F32)
    return pl.pallas_call(
        _gla_kernel,
        grid=(b, n),
        in_specs=[fwd(GLA_KW), fwd(GLA_KW), fwd(GLA_VW), fwd(GLA_KW),
                  bwd(GLA_KW), bwd(GLA_KW), bwd(GLA_VW), bwd(GLA_KW)],
        out_specs=[fwd(GLA_VW), bwd(GLA_VW)],
        out_shape=[out, out],
        scratch_shapes=[state, state],
        compiler_params=pltpu.CompilerParams(
            dimension_semantics=("parallel", "arbitrary"), vmem_limit_bytes=VMEM_LIMIT),
        name="gla",
    )(q, k, v, la_f, q, k, v, la_b)


def _out_ffn_kernel(x_ref, attn_ref, of_ref, ob_ref, og_ref, gn_ref, wo_ref, g2_ref,
                    wg_ref, wu_ref, wd_ref, gf_ref, y_ref):
    def rms(t, g):
        return t * lax.rsqrt(jnp.mean(t * t, axis=-1, keepdims=True) + EPS) * g

    groups = [slice(i * (TOKEN_TILE // ROW_SPLIT), (i + 1) * (TOKEN_TILE // ROW_SPLIT))
              for i in range(ROW_SPLIT)]
    mixed = []
    for rs in groups:
        o = of_ref[rs, :] + ob_ref[rs, :]
        og = og_ref[rs, :].astype(F32)
        gate = og * (1.0 / (1.0 + jnp.exp(-og)))
        heads = []
        for h in range(GLA_HEADS):
            sl = slice(h * GLA_DV, (h + 1) * GLA_DV)
            heads.append(rms(o[:, sl], gn_ref[:, sl]) * gate[:, sl])
        mixed.append(jnp.concatenate([attn_ref[rs, :]] + [t.astype(BF16) for t in heads], axis=1))
    x1 = [x_ref[rs, :] + jnp.dot(m, wo_ref[...], preferred_element_type=F32)
          for rs, m in zip(groups, mixed)]
    h2 = [rms(t, g2_ref[...]).astype(BF16) for t in x1]
    ff = [jnp.zeros_like(t) for t in x1]
    for lo, hi in FF_CHUNKS:
        cs = slice(lo, hi)
        a = [jnp.dot(h, wg_ref[:, cs], preferred_element_type=F32) for h in h2]
        u = [jnp.dot(h, wu_ref[:, cs], preferred_element_type=F32) for h in h2]
        act = [(ai * (1.0 / (1.0 + jnp.exp(-ai))) * ui).astype(BF16) for ai, ui in zip(a, u)]
        ff = [f + jnp.dot(t, wd_ref[cs, :], preferred_element_type=F32) for f, t in zip(ff, act)]
    for rs, t, f in zip(groups, x1, ff):
        y_ref[rs, :] = rms(t + f, gf_ref[...])


def _out_ffn(x2, attn, o_f, o_b, og, gn, wo, g2, wg, wu, wd, gf):
    t = x2.shape[0]
    tm = TOKEN_TILE
    row = lambda i: (i, 0)
    return pl.pallas_call(
        _out_ffn_kernel,
        grid=(t // tm,),
        in_specs=[
            pl.BlockSpec((tm, D_MODEL), row),
            pl.BlockSpec((tm, ATTN_WIDTH), row),
            pl.BlockSpec((tm, GLA_VW), row),
            pl.BlockSpec((tm, GLA_VW), row),
            pl.BlockSpec((tm, GLA_VW), row),
            _const_spec((1, GLA_VW)),
            _const_spec((D_MODEL, D_MODEL)),
            _const_spec((1, D_MODEL)),
            _const_spec((D_MODEL, D_FF)),
            _const_spec((D_MODEL, D_FF)),
            _const_spec((D_FF, D_MODEL)),
            _const_spec((1, D_MODEL)),
        ],
        out_specs=pl.BlockSpec((tm, D_MODEL), row),
        out_shape=jax.ShapeDtypeStruct((t, D_MODEL), F32),
        compiler_params=pltpu.CompilerParams(
            dimension_semantics=("parallel",), vmem_limit_bytes=VMEM_LIMIT),
        name="out_ffn",
    )(x2, attn, o_f, o_b, og, gn, wo, g2, wg, wu, wd, gf)


def _rope_tables(seq):
    inv_freq = ROPE_THETA ** (-jnp.arange(0, HEAD_DIM, 2, dtype=F32) / HEAD_DIM)
    ang = jnp.arange(seq, dtype=F32)[:, None] * inv_freq[None, :]
    cos = jnp.tile(jnp.cos(ang), (1, LANES // (HEAD_DIM // 2)))
    sin = jnp.tile(jnp.sin(ang), (1, LANES // (HEAD_DIM // 2)))
    second_half = (jnp.arange(LANES) % HEAD_DIM) >= HEAD_DIM // 2
    sina = jnp.where(second_half[None, :], sin, 0.0)
    sinb = jnp.where(second_half[None, :], 0.0, -sin)
    return cos, sina, sinb


def _trunk(x, p):
    b, seq, _ = x.shape
    assert seq % SUPER_BLOCK == 0 and seq // DILATIONS[-1] >= K_WINDOW
    assert seq % GLA_BLOCK == 0 and (b * seq) % TOKEN_TILE == 0 and seq % TOKEN_TILE == 0
    x2 = x.reshape(b * seq, D_MODEL)
    cos, sina, sinb = _rope_tables(seq)
    qa, ka, va, qg, kg, vg, og, la_f, la_b = _inproj(
        x2, seq, p["g1"], p["w_in"], cos, sina, sinb, p["gate_up"], p["gate_bias"])
    r3 = lambda a: a.reshape(b, seq, a.shape[-1])
    attn = _attention(r3(qa), r3(ka), r3(va))
    o_f, o_b = _gla(r3(qg), r3(kg), r3(vg), r3(la_f), r3(la_b))
    r2 = lambda a: a.reshape(b * seq, a.shape[-1])
    y = _out_ffn(x2, r2(attn), r2(o_f), r2(o_b), og, p["gn"], p["w_out"], p["g2"],
                 p["w_gate"], p["w_up"], p["w_down"], p["gf"])
    return y.reshape(b, seq, D_MODEL)


def kernel(x_prompt, x_sample, norm1_g, w_in, gate_up_fwd, gate_bias_fwd, gate_up_bwd, gate_bias_bwd,
           gla_norm_g, w_out, norm2_g, w_gate, w_up, w_down, final_norm_g):
    assert norm1_g.shape[0] == 1, "single layer"
    w_in_p = jnp.pad(w_in[0], ((0, 0), (0, IN_COLS_PAD - IN_COLS))).astype(BF16)
    up = jnp.zeros((LANES, 2 * GLA_KW), F32)
    up = up.at[:GLA_GATE_RANK, :GLA_KW].set(gate_up_fwd[0])
    up = up.at[GLA_GATE_RANK:2 * GLA_GATE_RANK, GLA_KW:].set(gate_up_bwd[0])
    p = {
        "g1": norm1_g[0][None, :],
        "w_in": w_in_p,
        "gate_up": up.astype(BF16),
        "gate_bias": jnp.concatenate([gate_bias_fwd[0], gate_bias_bwd[0]])[None, :],
        "gn": gla_norm_g[0][None, :],
        "w_out": w_out[0].astype(BF16),
        "g2": norm2_g[0][None, :],
        "w_gate": w_gate[0].astype(BF16),
        "w_up": w_up[0].astype(BF16),
        "w_down": w_down[0].astype(BF16),
        "gf": final_norm_g[None, :],
    }
    return _trunk(x_prompt, p), _trunk(x_sample, p)
```

```python
import functools

import jax
import jax.numpy as jnp
from jax import lax
from jax.experimental import pallas as pl
from jax.experimental.pallas import tpu as pltpu

F32 = jnp.float32
BF16 = jnp.bfloat16

D_MODEL = 1024
HEAD_DIM = 64
ATTN_HEADS = 8
ATTN_WIDTH = ATTN_HEADS * HEAD_DIM
DILATIONS = (1, 4, 16)
HALF_WINDOW = 64
ROPE_THETA = 10000.0
GLA_HEADS = 4
GLA_DK = 64
GLA_DV = 128
GLA_KW = GLA_HEADS * GLA_DK
GLA_VW = GLA_HEADS * GLA_DV
GLA_GATE_RANK = 16
GLA_TAU = 16.0
GLA_CHUNK = 64
D_FF = 2816
EPS = 1e-6
MASK_VALUE = -1e30
LOG2_E = 1.4426950408889634

LANES = 128
IN_COLS = 3 * ATTN_WIDTH + 2 * GLA_KW + 2 * GLA_VW + 2 * GLA_GATE_RANK
IN_COLS_PAD = 3200
GATE_COL0 = 3 * ATTN_WIDTH + 2 * GLA_KW + 2 * GLA_VW
VMEM_LIMIT = 56 * 1024 * 1024

TOKEN_TILE = 512
Q_BLOCK = 128
K_WINDOW = Q_BLOCK + 2 * HALF_WINDOW
SUPER_BLOCK = Q_BLOCK * DILATIONS[-1]
ATTN_UNROLL = 8
GLA_BLOCK = 512
FF_CHUNKS = ((0, 1536), (1536, D_FF))
ROW_SPLIT = 2


def _const_spec(shape):
    return pl.BlockSpec(shape, lambda *_: (0,) * len(shape), pipeline_mode=pl.Buffered(1))


def _inproj_kernel(x_ref, g_ref, w_ref, cos_ref, sina_ref, sinb_ref, up_ref, gb_ref,
                   qa_ref, ka_ref, va_ref, qg_ref, kg_ref, vg_ref, og_ref, laf_ref, lab_ref):
    groups = [slice(i * (TOKEN_TILE // ROW_SPLIT), (i + 1) * (TOKEN_TILE // ROW_SPLIT))
              for i in range(ROW_SPLIT)]
    hns = []
    for rs in groups:
        x = x_ref[rs, :]
        ms = jnp.mean(x * x, axis=-1, keepdims=True)
        hns.append((x * lax.rsqrt(ms + EPS) * g_ref[...]).astype(BF16))

    def proj(hn, col0, width):
        return jnp.dot(hn, w_ref[:, col0:col0 + width], preferred_element_type=F32)

    def rope(t, rs):
        cos = cos_ref[rs, :]
        sina = sina_ref[rs, :]
        sinb = sinb_ref[rs, :]
        out = []
        for j in range(t.shape[1] // LANES):
            s = t[:, j * LANES:(j + 1) * LANES]
            out.append(s * cos + pltpu.roll(s, HEAD_DIM // 2, 1) * sina
                       + pltpu.roll(s, LANES - HEAD_DIM // 2, 1) * sinb)
        return jnp.concatenate(out, axis=1)

    c = 3 * ATTN_WIDTH
    for rs, hn in zip(groups, hns):
        r = proj(hn, GATE_COL0, LANES).astype(BF16)
        z = jnp.dot(r, up_ref[...], preferred_element_type=F32) + gb_ref[...]
        la = (jnp.minimum(z, 0.0) - jnp.log(1.0 + jnp.exp(-jnp.abs(z)))) * (1.0 / GLA_TAU)
        laf_ref[rs, :] = la[:, :GLA_KW]
        lab_ref[rs, :] = la[:, GLA_KW:]
    for rs, hn in zip(groups, hns):
        qa_ref[rs, :] = rope(proj(hn, 0, ATTN_WIDTH), rs) * (HEAD_DIM ** -0.5 * LOG2_E)
    for rs, hn in zip(groups, hns):
        ka_ref[rs, :] = rope(proj(hn, ATTN_WIDTH, ATTN_WIDTH), rs)
    for rs, hn in zip(groups, hns):
        va_ref[rs, :] = proj(hn, 2 * ATTN_WIDTH, ATTN_WIDTH)
    for rs, hn in zip(groups, hns):
        qg_ref[rs, :] = proj(hn, c, GLA_KW) * (GLA_DK ** -0.5)
    for rs, hn in zip(groups, hns):
        kg_ref[rs, :] = proj(hn, c + GLA_KW, GLA_KW)
    for rs, hn in zip(groups, hns):
        vg_ref[rs, :] = proj(hn, c + 2 * GLA_KW, GLA_VW).astype(BF16)
    for rs, hn in zip(groups, hns):
        og_ref[rs, :] = proj(hn, c + 2 * GLA_KW + GLA_VW, GLA_VW).astype(BF16)


def _inproj(x2, seq, g, w, cos, sina, sinb, up, gb):
    t = x2.shape[0]
    tm = TOKEN_TILE
    n_pos = seq // tm
    row = lambda i: (i, 0)
    pos = lambda i: (i % n_pos, 0)
    out_shapes = [
        jax.ShapeDtypeStruct((t, ATTN_WIDTH), F32),
        jax.ShapeDtypeStruct((t, ATTN_WIDTH), F32),
        jax.ShapeDtypeStruct((t, ATTN_WIDTH), F32),
        jax.ShapeDtypeStruct((t, GLA_KW), F32),
        jax.ShapeDtypeStruct((t, GLA_KW), F32),
        jax.ShapeDtypeStruct((t, GLA_VW), BF16),
        jax.ShapeDtypeStruct((t, GLA_VW), BF16),
        jax.ShapeDtypeStruct((t, GLA_KW), F32),
        jax.ShapeDtypeStruct((t, GLA_KW), F32),
    ]
    return pl.pallas_call(
        _inproj_kernel,
        grid=(t // tm,),
        in_specs=[
            pl.BlockSpec((tm, D_MODEL), row),
            _const_spec((1, D_MODEL)),
            _const_spec((D_MODEL, IN_COLS_PAD)),
            pl.BlockSpec((tm, LANES), pos),
            pl.BlockSpec((tm, LANES), pos),
            pl.BlockSpec((tm, LANES), pos),
            _const_spec((LANES, 2 * GLA_KW)),
            _const_spec((1, 2 * GLA_KW)),
        ],
        out_specs=[pl.BlockSpec((tm, s.shape[1]), row) for s in out_shapes],
        out_shape=out_shapes,
        compiler_params=pltpu.CompilerParams(
            dimension_semantics=("parallel",), vmem_limit_bytes=VMEM_LIMIT),
        name="inproj",
    )(x2, g, w, cos, sina, sinb, up, gb)


def _attn_kernel(q_ref, k_ref, v_ref, o_ref, o4_ref, lse4_ref, o16_ref, lse16_ref, bias_ref,
                 s_stage, p_stage, m_stage, *, seq):
    lane = lax.broadcasted_iota(jnp.int32, (Q_BLOCK, LANES), 1)
    head0 = lane < HEAD_DIM
    ones = jnp.ones((K_WINDOW, LANES), BF16)

    col_minus_row = (lax.broadcasted_iota(jnp.int32, (Q_BLOCK, K_WINDOW), 1)
                     - lax.broadcasted_iota(jnp.int32, (Q_BLOCK, K_WINDOW), 0))
    for e in range(bias_ref.shape[0]):
        bias_ref[e] = jnp.where(jnp.abs(col_minus_row - HALF_WINDOW * e) <= HALF_WINDOW,
                                0.0, MASK_VALUE)

    def window(d, cls, sub0):
        n_sub = seq // d
        win0 = jnp.clip(sub0 - HALF_WINDOW, 0, n_sub - K_WINDOW)
        if d > 1:
            q_rows = pl.ds(cls + d * sub0, Q_BLOCK, stride=d)
            k_rows = pl.ds(cls + d * win0, K_WINDOW, stride=d)
        else:
            q_rows = pl.ds(pl.multiple_of(sub0, Q_BLOCK), Q_BLOCK)
            k_rows = pl.ds(pl.multiple_of(win0, HALF_WINDOW), K_WINDOW)
        return q_rows, k_rows, (sub0 - win0) // HALF_WINDOW

    def patterns(units):
        wins = [window(*u) for u in units]
        for u, (q_rows, k_rows, case) in enumerate(wins):
            q = q_ref[q_rows, :]
            k = k_ref[k_rows, :].astype(BF16)
            q2 = jnp.concatenate([jnp.where(head0, q, 0.0), jnp.where(head0, 0.0, q)],
                                 axis=0).astype(BF16)
            s = lax.dot_general(q2, k, (((1,), (1,)), ((), ())), preferred_element_type=F32)
            bias = bias_ref[case]
            s_stage[u] = s + jnp.concatenate([bias, bias], axis=0)
        for u in range(len(units)):
            ms = []
            for h in range(2):
                sh = s_stage[u, h * Q_BLOCK:(h + 1) * Q_BLOCK, :]
                m = jnp.max(sh, axis=-1, keepdims=True)
                ms.append(jnp.broadcast_to(m, (Q_BLOCK, LANES)))
                p_stage[u, h * Q_BLOCK:(h + 1) * Q_BLOCK, :] = jnp.exp2(sh - m).astype(BF16)
            m_stage[u] = jnp.where(head0, ms[0], ms[1])
        res = []
        for u, (_, k_rows, _) in enumerate(wins):
            v = jnp.concatenate([v_ref[k_rows, :].astype(BF16), ones], axis=1)
            pv = jnp.dot(p_stage[u], v, preferred_element_type=F32)
            l = jnp.where(head0, pv[:Q_BLOCK, LANES:], pv[Q_BLOCK:, LANES:])
            out = jnp.where(head0, pv[:Q_BLOCK, :LANES], pv[Q_BLOCK:, :LANES]) / l
            res.append((out, m_stage[u] + jnp.log2(l)))
        return res

    n_iter = SUPER_BLOCK // Q_BLOCK // ATTN_UNROLL

    def super_block(sb, carry):
        base = pl.multiple_of(sb * SUPER_BLOCK, SUPER_BLOCK)

        for d, o_d, lse_d in ((16, o16_ref, lse16_ref), (4, o4_ref, lse4_ref)):
            blocks_per_class = SUPER_BLOCK // d // Q_BLOCK

            def park(it, c, d=d, bpc=blocks_per_class, o_d=o_d, lse_d=lse_d):
                idx = [it * ATTN_UNROLL + u for u in range(ATTN_UNROLL)]
                cls = [i // bpc for i in idx]
                blk = [(i % bpc) * Q_BLOCK for i in idx]
                res = patterns([(d, cl, sb * (SUPER_BLOCK // d) + bl)
                                for cl, bl in zip(cls, blk)])
                for (out, lse), cl, bl in zip(res, cls, blk):
                    rows = pl.ds(cl + d * bl, Q_BLOCK, stride=d)
                    o_d[rows, :] = out
                    lse_d[rows, :] = lse
                return c

            lax.fori_loop(0, n_iter, park, 0)

        def merge(it, c):
            blk = [pl.multiple_of((it * ATTN_UNROLL + u) * Q_BLOCK, Q_BLOCK)
                   for u in range(ATTN_UNROLL)]
            res = patterns([(1, 0, base + bl) for bl in blk])
            for (out1, lse1), bl in zip(res, blk):
                rows = pl.ds(bl, Q_BLOCK)
                lse4 = lse4_ref[rows, :]
                lse16 = lse16_ref[rows, :]
                top = jnp.maximum(jnp.maximum(lse1, lse4), lse16)
                w1 = jnp.exp2(lse1 - top)
                w4 = jnp.exp2(lse4 - top)
                w16 = jnp.exp2(lse16 - top)
                mix = ((w1 * out1 + w4 * o4_ref[rows, :] + w16 * o16_ref[rows, :])
                       / (w1 + w4 + w16))
                o_ref[pl.ds(base + bl, Q_BLOCK), :] = mix.astype(o_ref.dtype)
            return c

        lax.fori_loop(0, n_iter, merge, 0)
        return carry

    lax.fori_loop(0, seq // SUPER_BLOCK, super_block, 0)


def _attention(q, k, v):
    b, seq, _ = q.shape
    spec = pl.BlockSpec((None, seq, LANES), lambda i, h: (i, 0, h))
    stat = pltpu.VMEM((SUPER_BLOCK, LANES), F32)
    return pl.pallas_call(
        functools.partial(_attn_kernel, seq=seq),
        grid=(b, ATTN_WIDTH // LANES),
        in_specs=[spec, spec, spec],
        out_specs=spec,
        out_shape=jax.ShapeDtypeStruct((b, seq, ATTN_WIDTH), BF16),
        scratch_shapes=[stat, stat, stat, stat, pltpu.VMEM((3, Q_BLOCK, K_WINDOW), F32),
                        pltpu.VMEM((ATTN_UNROLL, 2 * Q_BLOCK, K_WINDOW), F32),
                        pltpu.VMEM((ATTN_UNROLL, 2 * Q_BLOCK, K_WINDOW), BF16),
                        pltpu.VMEM((ATTN_UNROLL, Q_BLOCK, LANES), F32)],
        compiler_params=pltpu.CompilerParams(
            dimension_semantics=("parallel", "parallel"), vmem_limit_bytes=VMEM_LIMIT),
        name="dilated_attn",
    )(q, k, v)


def _gla_kernel(qf_ref, kf_ref, vf_ref, laf_ref, qb_ref, kb_ref, vb_ref, lab_ref,
                of_ref, ob_ref, stf_ref, stb_ref):
    c = GLA_CHUNK

    @pl.when(pl.program_id(1) == 0)
    def _():
        stf_ref[...] = jnp.zeros_like(stf_ref)
        stb_ref[...] = jnp.zeros_like(stb_ref)

    row = lax.broadcasted_iota(jnp.int32, (c, 2 * c), 0)
    col = lax.broadcasted_iota(jnp.int32, (c, 2 * c), 1) % c
    tri_f = jnp.where(col <= row, 1.0, 0.0).astype(BF16)
    tri_b = jnp.where(col >= row, 1.0, 0.0).astype(BF16)
    row4 = lax.broadcasted_iota(jnp.int32, (c, GLA_HEADS * c), 0)
    col4 = lax.broadcasted_iota(jnp.int32, (c, GLA_HEADS * c), 1) % c
    keep_f = col4 <= row4
    keep_b = col4 > row4
    klane = lax.broadcasted_iota(jnp.int32, (c, GLA_KW), 1) // GLA_DK
    zeros_v = jnp.zeros((c, GLA_DV), BF16)
    nt = (((1,), (1,)), ((), ()))
    tn = (((0,), (0,)), ((), ()))

    def by_head(t, lane_head):
        return jnp.concatenate(
            [jnp.where(lane_head == h, t, 0.0) for h in range(GLA_HEADS)], axis=0).astype(BF16)

    n_chunks = GLA_BLOCK // c
    streams = (
        (qf_ref, kf_ref, vf_ref, laf_ref, of_ref, stf_ref, tri_f, keep_f, c - 1,
         [j * c for j in range(n_chunks)]),
        (qb_ref, kb_ref, vb_ref, lab_ref, ob_ref, stb_ref, tri_b, keep_b, 0,
         [(n_chunks - 1 - j) * c for j in range(n_chunks)]),
    )
    jobs = [(s, streams[s][9][j]) for j in range(n_chunks) for s in range(2)]

    def rows(ref, r0):
        return ref[r0:r0 + c, :]

    cums = []
    for s, r0 in jobs:
        la = rows(streams[s][3], r0)
        la_hi = la.astype(BF16)
        la_lo = (la - la_hi.astype(F32)).astype(BF16)
        cums.append(jnp.dot(streams[s][6], jnp.concatenate([la_hi, la_lo], axis=0),
                            preferred_element_type=F32))

    q_es, decays, attns, kv_ts = [], [], [], []
    for (s, r0), cum in zip(jobs, cums):
        q_ref, k_ref, v_ref = streams[s][0:3]
        last_row = streams[s][8]
        cum_last = cum[last_row:last_row + 1, :]
        k = rows(k_ref, r0)
        q_e_f32 = rows(q_ref, r0) * jnp.exp(cum)
        q_e = q_e_f32.astype(BF16)
        k_e = k * jnp.exp(-cum)
        k_end = k * jnp.exp(cum_last - cum)
        q_es.append(by_head(q_e_f32, klane))
        decays.append(jnp.exp(cum_last))
        attns.append(lax.dot_general(q_e, by_head(k_e, klane), nt, preferred_element_type=F32))
        v = rows(v_ref, r0)
        v_heads = jnp.concatenate(
            [v[:, h * GLA_DV:(h + 1) * GLA_DV] for h in range(GLA_HEADS)], axis=0)
        kv_ts.append(lax.dot_general(v_heads, by_head(k_end, klane), tn,
                                     preferred_element_type=F32))

    o_intras = []
    for (s, r0), attn in zip(jobs, attns):
        v = rows(streams[s][2], r0)
        v_h = [v[:, h * GLA_DV:(h + 1) * GLA_DV] for h in range(GLA_HEADS)]
        v_diag = jnp.concatenate(
            [jnp.concatenate([v_h[h] if h == g else zeros_v for h in range(GLA_HEADS)], axis=0)
             for g in range(GLA_HEADS)], axis=1)
        attn = jnp.where(streams[s][7], attn, 0.0).astype(BF16)
        o_intras.append(jnp.dot(attn, v_diag, preferred_element_type=F32))

    st = [streams[0][5][...], streams[1][5][...]]
    for (s, r0), q_heads, decay, kv_t, o_intra in zip(jobs, q_es, decays, kv_ts, o_intras):
        o_state = lax.dot_general(q_heads, st[s].astype(BF16), nt, preferred_element_type=F32)
        streams[s][4][r0:r0 + c, :] = o_intra + jnp.concatenate(
            [o_state[h * c:(h + 1) * c] for h in range(GLA_HEADS)], axis=1)
        st[s] = st[s] * decay + kv_t
    streams[0][5][...] = st[0]
    streams[1][5][...] = st[1]


def _gla(q, k, v, la_f, la_b):
    b, seq, _ = q.shape
    n = seq // GLA_BLOCK
    fwd = lambda w: pl.BlockSpec((None, GLA_BLOCK, w), lambda i, j: (i, j, 0))
    bwd = lambda w: pl.BlockSpec((None, GLA_BLOCK, w), lambda i, j: (i, n - 1 - j, 0))
    out = jax.ShapeDtypeStruct((b, seq, GLA_VW), F32)
    state = pltpu.VMEM((GLA_DV, GLA_KW), F32)
    return pl.pallas_call(
        _gla_kernel,
        grid=(b, n),
        in_specs=[fwd(GLA_KW), fwd(GLA_KW), fwd(GLA_VW), fwd(GLA_KW),
                  bwd(GLA_KW), bwd(GLA_KW), bwd(GLA_VW), bwd(GLA_KW)],
        out_specs=[fwd(GLA_VW), bwd(GLA_VW)],
        out_shape=[out, out],
        scratch_shapes=[state, state],
        compiler_params=pltpu.CompilerParams(
            dimension_semantics=("parallel", "arbitrary"), vmem_limit_bytes=VMEM_LIMIT),
        name="gla",
    )(q, k, v, la_f, q, k, v, la_b)


def _out_ffn_kernel(x_ref, attn_ref, of_ref, ob_ref, og_ref, gn_ref, wo_ref, g2_ref,
                    wg_ref, wu_ref, wd_ref, gf_ref, y_ref):
    def rms(t, g):
        return t * lax.rsqrt(jnp.mean(t * t, axis=-1, keepdims=True) + EPS) * g

    groups = [slice(i * (TOKEN_TILE // ROW_SPLIT), (i + 1) * (TOKEN_TILE // ROW_SPLIT))
              for i in range(ROW_SPLIT)]
    mixed = []
    for rs in groups:
        o = of_ref[rs, :] + ob_ref[rs, :]
        og = og_ref[rs, :].astype(F32)
        gate = og * (1.0 / (1.0 + jnp.exp(-og)))
        heads = []
        for h in range(GLA_HEADS):
            sl = slice(h * GLA_DV, (h + 1) * GLA_DV)
            heads.append(rms(o[:, sl], gn_ref[:, sl]) * gate[:, sl])
        mixed.append(jnp.concatenate([attn_ref[rs, :]] + [t.astype(BF16) for t in heads], axis=1))
    x1 = [x_ref[rs, :] + jnp.dot(m, wo_ref[...], preferred_element_type=F32)
          for rs, m in zip(groups, mixed)]
    h2 = [rms(t, g2_ref[...]).astype(BF16) for t in x1]
    ff = [jnp.zeros_like(t) for t in x1]
    for lo, hi in FF_CHUNKS:
        cs = slice(lo, hi)
        a = [jnp.dot(h, wg_ref[:, cs], preferred_element_type=F32) for h in h2]
        u = [jnp.dot(h, wu_ref[:, cs], preferred_element_type=F32) for h in h2]
        act = [(ai * (1.0 / (1.0 + jnp.exp(-ai))) * ui).astype(BF16) for ai, ui in zip(a, u)]
        ff = [f + jnp.dot(t, wd_ref[cs, :], preferred_element_type=F32) for f, t in zip(ff, act)]
    for rs, t, f in zip(groups, x1, ff):
        y_ref[rs, :] = rms(t + f, gf_ref[...])


def _out_ffn(x2, attn, o_f, o_b, og, gn, wo, g2, wg, wu, wd, gf):
    t = x2.shape[0]
    tm = TOKEN_TILE
    row = lambda i: (i, 0)
    return pl.pallas_call(
        _out_ffn_kernel,
        grid=(t // tm,),
        in_specs=[
            pl.BlockSpec((tm, D_MODEL), row),
            pl.BlockSpec((tm, ATTN_WIDTH), row),
            pl.BlockSpec((tm, GLA_VW), row),
            pl.BlockSpec((tm, GLA_VW), row),
            pl.BlockSpec((tm, GLA_VW), row),
            _const_spec((1, GLA_VW)),
            _const_spec((D_MODEL, D_MODEL)),
            _const_spec((1, D_MODEL)),
            _const_spec((D_MODEL, D_FF)),
            _const_spec((D_MODEL, D_FF)),
            _const_spec((D_FF, D_MODEL)),
            _const_spec((1, D_MODEL)),
        ],
        out_specs=pl.BlockSpec((tm, D_MODEL), row),
        out_shape=jax.ShapeDtypeStruct((t, D_MODEL), F32),
        compiler_params=pltpu.CompilerParams(
            dimension_semantics=("parallel",), vmem_limit_bytes=VMEM_LIMIT),
        name="out_ffn",
    )(x2, attn, o_f, o_b, og, gn, wo, g2, wg, wu, wd, gf)


def _rope_tables(seq):
    inv_freq = ROPE_THETA ** (-jnp.arange(0, HEAD_DIM, 2, dtype=F32) / HEAD_DIM)
    ang = jnp.arange(seq, dtype=F32)[:, None] * inv_freq[None, :]
    cos = jnp.tile(jnp.cos(ang), (1, LANES // (HEAD_DIM // 2)))
    sin = jnp.tile(jnp.sin(ang), (1, LANES // (HEAD_DIM // 2)))
    second_half = (jnp.arange(LANES) % HEAD_DIM) >= HEAD_DIM // 2
    sina = jnp.where(second_half[None, :], sin, 0.0)
    sinb = jnp.where(second_half[None, :], 0.0, -sin)
    return cos, sina, sinb


def _trunk(x, p):
    b, seq, _ = x.shape
    assert seq % SUPER_BLOCK == 0 and seq // DILATIONS[-1] >= K_WINDOW
    assert seq % GLA_BLOCK == 0 and (b * seq) % TOKEN_TILE == 0 and seq % TOKEN_TILE == 0
    x2 = x.reshape(b * seq, D_MODEL)
    cos, sina, sinb = _rope_tables(seq)
    qa, ka, va, qg, kg, vg, og, la_f, la_b = _inproj(
        x2, seq, p["g1"], p["w_in"], cos, sina, sinb, p["gate_up"], p["gate_bias"])
    r3 = lambda a: a.reshape(b, seq, a.shape[-1])
    attn = _attention(r3(qa), r3(ka), r3(va))
    o_f, o_b = _gla(r3(qg), r3(kg), r3(vg), r3(la_f), r3(la_b))
    r2 = lambda a: a.reshape(b * seq, a.shape[-1])
    y = _out_ffn(x2, r2(attn), r2(o_f), r2(o_b), og, p["gn"], p["w_out"], p["g2"],
                 p["w_gate"], p["w_up"], p["w_down"], p["gf"])
    return y.reshape(b, seq, D_MODEL)


def kernel(x_prompt, x_sample, norm1_g, w_in, gate_up_fwd, gate_bias_fwd, gate_up_bwd, gate_bias_bwd,
           gla_norm_g, w_out, norm2_g, w_gate, w_up, w_down, final_norm_g):
    assert norm1_g.shape[0] == 1, "single layer"
    w_in_p = jnp.pad(w_in[0], ((0, 0), (0, IN_COLS_PAD - IN_COLS))).astype(BF16)
    up = jnp.zeros((LANES, 2 * GLA_KW), F32)
    up = up.at[:GLA_GATE_RANK, :GLA_KW].set(gate_up_fwd[0])
    up = up.at[GLA_GATE_RANK:2 * GLA_GATE_RANK, GLA_KW:].set(gate_up_bwd[0])
    p = {
        "g1": norm1_g[0][None, :],
        "w_in": w_in_p,
        "gate_up": up.astype(BF16),
        "gate_bias": jnp.concatenate([gate_bias_fwd[0], gate_bias_bwd[0]])[None, :],
        "gn": gla_norm_g[0][None, :],
        "w_out": w_out[0].astype(BF16),
        "g2": norm2_g[0][None, :],
        "w_gate": w_gate[0].astype(BF16),
        "w_up": w_up[0].astype(BF16),
        "w_down": w_down[0].astype(BF16),
        "gf": final_norm_g[None, :],
    }
    return _trunk(x_prompt, p), _trunk(x_sample, p)
```

```python
import functools

import jax
import jax.numpy as jnp
from jax import lax
from jax.experimental import pallas as pl
from jax.experimental.pallas import tpu as pltpu

F32 = jnp.float32
BF16 = jnp.bfloat16

D_MODEL = 1024
HEAD_DIM = 64
ATTN_HEADS = 8
ATTN_WIDTH = ATTN_HEADS * HEAD_DIM
DILATIONS = (1, 4, 16)
HALF_WINDOW = 64
ROPE_THETA = 10000.0
GLA_HEADS = 4
GLA_DK = 64
GLA_DV = 128
GLA_KW = GLA_HEADS * GLA_DK
GLA_VW = GLA_HEADS * GLA_DV
GLA_GATE_RANK = 16
GLA_TAU = 16.0
GLA_CHUNK = 64
D_FF = 2816
EPS = 1e-6
MASK_VALUE = -1e30
LOG2_E = 1.4426950408889634

LANES = 128
IN_COLS = 3 * ATTN_WIDTH + 2 * GLA_KW + 2 * GLA_VW + 2 * GLA_GATE_RANK
IN_COLS_PAD = 3200
GATE_COL0 = 3 * ATTN_WIDTH + 2 * GLA_KW + 2 * GLA_VW
VMEM_LIMIT = 56 * 1024 * 1024

TOKEN_TILE = 512
Q_BLOCK = 128
K_WINDOW = Q_BLOCK + 2 * HALF_WINDOW
SUPER_BLOCK = Q_BLOCK * DILATIONS[-1]
ATTN_UNROLL = 8
GLA_BLOCK = 512
FF_CHUNKS = ((0, 1536), (1536, D_FF))
ROW_SPLIT = 2


def _const_spec(shape):
    return pl.BlockSpec(shape, lambda *_: (0,) * len(shape), pipeline_mode=pl.Buffered(1))


def _inproj_kernel(x_ref, g_ref, w_ref, cos_ref, sina_ref, sinb_ref, up_ref, gb_ref,
                   q1_ref, q4_ref, q16_ref, k1_ref, k4_ref, k16_ref, v1_ref, v4_ref, v16_ref,
                   qg_ref, kg_ref, vg_ref, og_ref, laf_ref, lab_ref,
                   q_stage, k_stage, v_stage, q_stage4, k_stage4, v_stage4):
    groups = [slice(i * (TOKEN_TILE // ROW_SPLIT), (i + 1) * (TOKEN_TILE // ROW_SPLIT))
              for i in range(ROW_SPLIT)]
    hns = []
    for rs in groups:
        x = x_ref[rs, :]
        ms = jnp.mean(x * x, axis=-1, keepdims=True)
        hns.append((x * lax.rsqrt(ms + EPS) * g_ref[...]).astype(BF16))

    def proj(hn, col0, width):
        return jnp.dot(hn, w_ref[:, col0:col0 + width], preferred_element_type=F32)

    def rope(t, rs):
        cos = cos_ref[rs, :]
        sina = sina_ref[rs, :]
        sinb = sinb_ref[rs, :]
        out = []
        for j in range(t.shape[1] // LANES):
            s = t[:, j * LANES:(j + 1) * LANES]
            out.append(s * cos + pltpu.roll(s, HEAD_DIM // 2, 1) * sina
                       + pltpu.roll(s, LANES - HEAD_DIM // 2, 1) * sinb)
        return jnp.concatenate(out, axis=1)

    def emit(t, rs, stage, stage4, outs):
        out1, out4, out16 = outs
        n = rs.stop - rs.start
        q4 = slice(rs.start // 4, rs.stop // 4)
        q16 = slice(rs.start // 16, rs.stop // 16)
        for j in range(ATTN_WIDTH // LANES):
            slab = t[:, j * LANES:(j + 1) * LANES]
            stage[j, rs, :] = slab
            out1[j, rs, :] = slab.astype(BF16)
            for r in range(4):
                piece = stage[j, pl.ds(rs.start + r, n // 4, stride=4), :]
                out4[j, r, q4, :] = piece.astype(BF16)
                stage4[j, r, q4, :] = piece
            for r16 in range(16):
                piece = stage4[j, r16 % 4, pl.ds(q4.start + r16 // 4, n // 16, stride=4), :]
                out16[j, r16, q16, :] = piece.astype(BF16)

    c = 3 * ATTN_WIDTH
    for rs, hn in zip(groups, hns):
        r = proj(hn, GATE_COL0, LANES).astype(BF16)
        z = jnp.dot(r, up_ref[...], preferred_element_type=F32) + gb_ref[...]
        la = (jnp.minimum(z, 0.0) - jnp.log(1.0 + jnp.exp(-jnp.abs(z)))) * (1.0 / GLA_TAU)
        laf_ref[rs, :] = la[:, :GLA_KW]
        lab_ref[rs, :] = la[:, GLA_KW:]
    for rs, hn in zip(groups, hns):
        emit(rope(proj(hn, 0, ATTN_WIDTH), rs) * (HEAD_DIM ** -0.5 * LOG2_E), rs,
             q_stage, q_stage4, (q1_ref, q4_ref, q16_ref))
    for rs, hn in zip(groups, hns):
        emit(rope(proj(hn, ATTN_WIDTH, ATTN_WIDTH), rs), rs, k_stage, k_stage4,
             (k1_ref, k4_ref, k16_ref))
    for rs, hn in zip(groups, hns):
        emit(proj(hn, 2 * ATTN_WIDTH, ATTN_WIDTH), rs, v_stage, v_stage4,
             (v1_ref, v4_ref, v16_ref))
    for rs, hn in zip(groups, hns):
        qg_ref[rs, :] = proj(hn, c, GLA_KW) * (GLA_DK ** -0.5)
    for rs, hn in zip(groups, hns):
        kg_ref[rs, :] = proj(hn, c + GLA_KW, GLA_KW)
    for rs, hn in zip(groups, hns):
        vg_ref[rs, :] = proj(hn, c + 2 * GLA_KW, GLA_VW).astype(BF16)
    for rs, hn in zip(groups, hns):
        og_ref[rs, :] = proj(hn, c + 2 * GLA_KW + GLA_VW, GLA_VW).astype(BF16)


def _inproj(x2, seq, g, w, cos, sina, sinb, up, gb):
    t = x2.shape[0]
    b = t // seq
    tm = TOKEN_TILE
    n_pos = seq // tm
    n_pair = ATTN_WIDTH // LANES
    row = lambda i: (i, 0)
    pos = lambda i: (i % n_pos, 0)
    attn_shapes, attn_specs = [], []
    for _ in range(3):
        for d in DILATIONS:
            if d == 1:
                attn_shapes.append(jax.ShapeDtypeStruct((n_pair, t, LANES), BF16))
                attn_specs.append(pl.BlockSpec((n_pair, tm, LANES), lambda i: (0, i, 0)))
            else:
                attn_shapes.append(jax.ShapeDtypeStruct((n_pair, b, d, seq // d, LANES), BF16))
                attn_specs.append(pl.BlockSpec((n_pair, None, d, tm // d, LANES),
                                               lambda i: (0, i // n_pos, 0, i % n_pos, 0)))
    out_shapes = attn_shapes + [
        jax.ShapeDtypeStruct((t, GLA_KW), F32),
        jax.ShapeDtypeStruct((t, GLA_KW), F32),
        jax.ShapeDtypeStruct((t, GLA_VW), BF16),
        jax.ShapeDtypeStruct((t, GLA_VW), BF16),
        jax.ShapeDtypeStruct((t, GLA_KW), F32),
        jax.ShapeDtypeStruct((t, GLA_KW), F32),
    ]
    return pl.pallas_call(
        _inproj_kernel,
        grid=(t // tm,),
        in_specs=[
            pl.BlockSpec((tm, D_MODEL), row),
            _const_spec((1, D_MODEL)),
            _const_spec((D_MODEL, IN_COLS_PAD)),
            pl.BlockSpec((tm, LANES), pos),
            pl.BlockSpec((tm, LANES), pos),
            pl.BlockSpec((tm, LANES), pos),
            _const_spec((LANES, 2 * GLA_KW)),
            _const_spec((1, 2 * GLA_KW)),
        ],
        out_specs=attn_specs + [pl.BlockSpec((tm, s.shape[1]), row)
                                for s in out_shapes[len(attn_specs):]],
        out_shape=out_shapes,
        scratch_shapes=([pltpu.VMEM((n_pair, tm, LANES), F32)] * 3
                        + [pltpu.VMEM((n_pair, 4, tm // 4, LANES), F32)] * 3),
        compiler_params=pltpu.CompilerParams(
            dimension_semantics=("parallel",), vmem_limit_bytes=VMEM_LIMIT),
        name="inproj",
    )(x2, g, w, cos, sina, sinb, up, gb)


def _attn_kernel(q1_ref, q4_ref, q16_ref, k1_ref, k4_ref, k16_ref, v1_ref, v4_ref, v16_ref,
                 o_ref, o4_ref, lse4_ref, o16_ref, lse16_ref, bias_ref,
                 s_stage, p_stage, m_stage, *, seq):
    lane = lax.broadcasted_iota(jnp.int32, (Q_BLOCK, LANES), 1)
    head0 = lane < HEAD_DIM
    ones = jnp.ones((K_WINDOW, LANES), BF16)
    qkv = {1: (q1_ref, k1_ref, v1_ref), 4: (q4_ref, k4_ref, v4_ref),
           16: (q16_ref, k16_ref, v16_ref)}

    col_minus_row = (lax.broadcasted_iota(jnp.int32, (Q_BLOCK, K_WINDOW), 1)
                     - lax.broadcasted_iota(jnp.int32, (Q_BLOCK, K_WINDOW), 0))
    for e in range(bias_ref.shape[0]):
        bias_ref[e] = jnp.where(jnp.abs(col_minus_row - HALF_WINDOW * e) <= HALF_WINDOW,
                                0.0, MASK_VALUE)

    def window(d, cls, sub0):
        n_sub = seq // d
        win0 = jnp.clip(sub0 - HALF_WINDOW, 0, n_sub - K_WINDOW)
        q_rows = pl.ds(pl.multiple_of(sub0, Q_BLOCK), Q_BLOCK)
        k_rows = pl.ds(pl.multiple_of(win0, HALF_WINDOW), K_WINDOW)
        q_ref, k_ref, v_ref = qkv[d]
        if d > 1:
            load = lambda ref, rows: ref[cls, rows, :]
        else:
            load = lambda ref, rows: ref[rows, :]
        return (lambda: load(q_ref, q_rows), lambda: load(k_ref, k_rows),
                lambda: load(v_ref, k_rows), (sub0 - win0) // HALF_WINDOW)

    def patterns(units):
        wins = [window(*u) for u in units]
        for u, (load_q, load_k, _, case) in enumerate(wins):
            q = load_q()
            k = load_k()
            zero = jnp.zeros_like(q)
            q2 = jnp.concatenate([jnp.where(head0, q, zero), jnp.where(head0, zero, q)], axis=0)
            s = lax.dot_general(q2, k, (((1,), (1,)), ((), ())), preferred_element_type=F32)
            bias = bias_ref[case]
            s_stage[u] = s + jnp.concatenate([bias, bias], axis=0)
        for u in range(len(units)):
            ms = []
            for h in range(2):
                sh = s_stage[u, h * Q_BLOCK:(h + 1) * Q_BLOCK, :]
                m = jnp.max(sh, axis=-1, keepdims=True)
                ms.append(jnp.broadcast_to(m, (Q_BLOCK, LANES)))
                p_stage[u, h * Q_BLOCK:(h + 1) * Q_BLOCK, :] = jnp.exp2(sh - m).astype(BF16)
            m_stage[u] = jnp.where(head0, ms[0], ms[1])
        res = []
        for u, (_, _, load_v, _) in enumerate(wins):
            v = jnp.concatenate([load_v(), ones], axis=1)
            pv = jnp.dot(p_stage[u], v, preferred_element_type=F32)
            l = jnp.where(head0, pv[:Q_BLOCK, LANES:], pv[Q_BLOCK:, LANES:])
            out = jnp.where(head0, pv[:Q_BLOCK, :LANES], pv[Q_BLOCK:, :LANES]) / l
            res.append((out, m_stage[u] + jnp.log2(l)))
        return res

    n_iter = SUPER_BLOCK // Q_BLOCK // ATTN_UNROLL

    def super_block(sb, carry):
        base = pl.multiple_of(sb * SUPER_BLOCK, SUPER_BLOCK)

        for d, o_d, lse_d in ((16, o16_ref, lse16_ref), (4, o4_ref, lse4_ref)):
            blocks_per_class = SUPER_BLOCK // d // Q_BLOCK

            def park(it, c, d=d, bpc=blocks_per_class, o_d=o_d, lse_d=lse_d):
                idx = [it * ATTN_UNROLL + u for u in range(ATTN_UNROLL)]
                cls = [i // bpc for i in idx]
                blk = [(i % bpc) * Q_BLOCK for i in idx]
                res = patterns([(d, cl, sb * (SUPER_BLOCK // d) + bl)
                                for cl, bl in zip(cls, blk)])
                for (out, lse), cl, bl in zip(res, cls, blk):
                    rows = pl.ds(cl + d * bl, Q_BLOCK, stride=d)
                    o_d[rows, :] = out
                    lse_d[rows, :] = lse
                return c

            lax.fori_loop(0, n_iter, park, 0)

        def merge(it, c):
            blk = [pl.multiple_of((it * ATTN_UNROLL + u) * Q_BLOCK, Q_BLOCK)
                   for u in range(ATTN_UNROLL)]
            res = patterns([(1, 0, base + bl) for bl in blk])
            for (out1, lse1), bl in zip(res, blk):
                rows = pl.ds(bl, Q_BLOCK)
                lse4 = lse4_ref[rows, :]
                lse16 = lse16_ref[rows, :]
                top = jnp.maximum(jnp.maximum(lse1, lse4), lse16)
                w1 = jnp.exp2(lse1 - top)
                w4 = jnp.exp2(lse4 - top)
                w16 = jnp.exp2(lse16 - top)
                mix = ((w1 * out1 + w4 * o4_ref[rows, :] + w16 * o16_ref[rows, :])
                       / (w1 + w4 + w16))
                o_ref[pl.ds(base + bl, Q_BLOCK), :] = mix.astype(o_ref.dtype)
            return c

        lax.fori_loop(0, n_iter, merge, 0)
        return carry

    lax.fori_loop(0, seq // SUPER_BLOCK, super_block, 0)


def _attention(qkv, b, seq):
    in_specs = []
    for a in qkv:
        if a.ndim == 3:
            in_specs.append(pl.BlockSpec((None, seq, LANES), lambda i, h: (h, i, 0)))
        else:
            in_specs.append(pl.BlockSpec((None, None) + a.shape[2:],
                                         lambda i, h: (h, i, 0, 0, 0)))
    stat = pltpu.VMEM((SUPER_BLOCK, LANES), F32)
    return pl.pallas_call(
        functools.partial(_attn_kernel, seq=seq),
        grid=(b, ATTN_WIDTH // LANES),
        in_specs=in_specs,
        out_specs=pl.BlockSpec((None, seq, LANES), lambda i, h: (i, 0, h)),
        out_shape=jax.ShapeDtypeStruct((b, seq, ATTN_WIDTH), BF16),
        scratch_shapes=[stat, stat, stat, stat, pltpu.VMEM((3, Q_BLOCK, K_WINDOW), F32),
                        pltpu.VMEM((ATTN_UNROLL, 2 * Q_BLOCK, K_WINDOW), F32),
                        pltpu.VMEM((ATTN_UNROLL, 2 * Q_BLOCK, K_WINDOW), BF16),
                        pltpu.VMEM((ATTN_UNROLL, Q_BLOCK, LANES), F32)],
        compiler_params=pltpu.CompilerParams(
            dimension_semantics=("parallel", "parallel"), vmem_limit_bytes=VMEM_LIMIT),
        name="dilated_attn",
    )(*qkv)


def _gla_kernel(qf_ref, kf_ref, vf_ref, laf_ref, qb_ref, kb_ref, vb_ref, lab_ref,
                of_ref, ob_ref, stf_ref, stb_ref):
    c = GLA_CHUNK

    @pl.when(pl.program_id(1) == 0)
    def _():
        stf_ref[...] = jnp.zeros_like(stf_ref)
        stb_ref[...] = jnp.zeros_like(stb_ref)

    row = lax.broadcasted_iota(jnp.int32, (c, 2 * c), 0)
    col = lax.broadcasted_iota(jnp.int32, (c, 2 * c), 1) % c
    tri_f = jnp.where(col <= row, 1.0, 0.0).astype(BF16)
    tri_b = jnp.where(col >= row, 1.0, 0.0).astype(BF16)
    row4 = lax.broadcasted_iota(jnp.int32, (c, GLA_HEADS * c), 0)
    col4 = lax.broadcasted_iota(jnp.int32, (c, GLA_HEADS * c), 1) % c
    keep_f = col4 <= row4
    keep_b = col4 > row4
    klane = lax.broadcasted_iota(jnp.int32, (c, GLA_KW), 1) // GLA_DK
    zeros_v = jnp.zeros((c, GLA_DV), BF16)
    nt = (((1,), (1,)), ((), ()))
    tn = (((0,), (0,)), ((), ()))

    def by_head(t, lane_head):
        return jnp.concatenate(
            [jnp.where(lane_head == h, t, 0.0) for h in range(GLA_HEADS)], axis=0).astype(BF16)

    n_chunks = GLA_BLOCK // c
    streams = (
        (qf_ref, kf_ref, vf_ref, laf_ref, of_ref, stf_ref, tri_f, keep_f, c - 1,
         [j * c for j in range(n_chunks)]),
        (qb_ref, kb_ref, vb_ref, lab_ref, ob_ref, stb_ref, tri_b, keep_b, 0,
         [(n_chunks - 1 - j) * c for j in range(n_chunks)]),
    )
    jobs = [(s, streams[s][9][j]) for j in range(n_chunks) for s in range(2)]

    def rows(ref, r0):
        return ref[r0:r0 + c, :]

    cums = []
    for s, r0 in jobs:
        la = rows(streams[s][3], r0)
        la_hi = la.astype(BF16)
        la_lo = (la - la_hi.astype(F32)).astype(BF16)
        cums.append(jnp.dot(streams[s][6], jnp.concatenate([la_hi, la_lo], axis=0),
                            preferred_element_type=F32))

    q_es, decays, attns, kv_ts = [], [], [], []
    for (s, r0), cum in zip(jobs, cums):
        q_ref, k_ref, v_ref = streams[s][0:3]
        last_row = streams[s][8]
        cum_last = cum[last_row:last_row + 1, :]
        k = rows(k_ref, r0)
        q_e_f32 = rows(q_ref, r0) * jnp.exp(cum)
        q_e = q_e_f32.astype(BF16)
        k_e = k * jnp.exp(-cum)
        k_end = k * jnp.exp(cum_last - cum)
        q_es.append(by_head(q_e_f32, klane))
        decays.append(jnp.exp(cum_last))
        attns.append(lax.dot_general(q_e, by_head(k_e, klane), nt, preferred_element_type=F32))
        v = rows(v_ref, r0)
        v_heads = jnp.concatenate(
            [v[:, h * GLA_DV:(h + 1) * GLA_DV] for h in range(GLA_HEADS)], axis=0)
        kv_ts.append(lax.dot_general(v_heads, by_head(k_end, klane), tn,
                                     preferred_element_type=F32))

    o_intras = []
    for (s, r0), attn in zip(jobs, attns):
        v = rows(streams[s][2], r0)
        v_h = [v[:, h * GLA_DV:(h + 1) * GLA_DV] for h in range(GLA_HEADS)]
        v_diag = jnp.concatenate(
            [jnp.concatenate([v_h[h] if h == g else zeros_v for h in range(GLA_HEADS)], axis=0)
             for g in range(GLA_HEADS)], axis=1)
        attn = jnp.where(streams[s][7], attn, 0.0).astype(BF16)
        o_intras.append(jnp.dot(attn, v_diag, preferred_element_type=F32))

    st = [streams[0][5][...], streams[1][5][...]]
    for (s, r0), q_heads, decay, kv_t, o_intra in zip(jobs, q_es, decays, kv_ts, o_intras):
        o_state = lax.dot_general(q_heads, st[s].astype(BF16), nt, preferred_element_type=F32)
        streams[s][4][r0:r0 + c, :] = o_intra + jnp.concatenate(
            [o_state[h * c:(h + 1) * c] for h in range(GLA_HEADS)], axis=1)
        st[s] = st[s] * decay + kv_t
    streams[0][5][...] = st[0]
    streams[1][5][...] = st[1]


def _gla(q, k, v, la_f, la_b):
    b, seq, _ = q.shape
    n = seq // GLA_BLOCK
    fwd = lambda w: pl.BlockSpec((None, GLA_BLOCK, w), lambda i, j: (i, j, 0))
    bwd = lambda w: pl.BlockSpec((None, GLA_BLOCK, w), lambda i, j: (i, n - 1 - j, 0))
    out = jax.ShapeDtypeStruct((b, seq, GLA_VW), F32)
    state = pltpu.VMEM((GLA_DV, GLA_KW), F32)
    return pl.pallas_call(
        _gla_kernel,
        grid=(b, n),
        in_specs=[fwd(GLA_KW), fwd(GLA_KW), fwd(GLA_VW), fwd(GLA_KW),
                  bwd(GLA_KW), bwd(GLA_KW), bwd(GLA_VW), bwd(GLA_KW)],
        out_specs=[fwd(GLA_VW), bwd(GLA_VW)],
        out_shape=[out, out],
        scratch_shapes=[state, state],
        compiler_params=pltpu.CompilerParams(
            dimension_semantics=("parallel", "arbitrary"), vmem_limit_bytes=VMEM_LIMIT),
        name="gla",
    )(q, k, v, la_f, q, k, v, la_b)


def _out_ffn_kernel(x_ref, attn_ref, of_ref, ob_ref, og_ref, gn_ref, wo_ref, g2_ref,
                    wg_ref, wu_ref, wd_ref, gf_ref, y_ref):
    def rms(t, g):
        return t * lax.rsqrt(jnp.mean(t * t, axis=-1, keepdims=True) + EPS) * g

    groups = [slice(i * (TOKEN_TILE // ROW_SPLIT), (i + 1) * (TOKEN_TILE // ROW_SPLIT))
              for i in range(ROW_SPLIT)]
    mixed = []
    for rs in groups:
        o = of_ref[rs, :] + ob_ref[rs, :]
        og = og_ref[rs, :].astype(F32)
        gate = og * (1.0 / (1.0 + jnp.exp(-og)))
        heads = []
        for h in range(GLA_HEADS):
            sl = slice(h * GLA_DV, (h + 1) * GLA_DV)
            heads.append(rms(o[:, sl], gn_ref[:, sl]) * gate[:, sl])
        mixed.append(jnp.concatenate([attn_ref[rs, :]] + [t.astype(BF16) for t in heads], axis=1))
    x1 = [x_ref[rs, :] + jnp.dot(m, wo_ref[...], preferred_element_type=F32)
          for rs, m in zip(groups, mixed)]
    h2 = [rms(t, g2_ref[...]).astype(BF16) for t in x1]
    ff = [jnp.zeros_like(t) for t in x1]
    for lo, hi in FF_CHUNKS:
        cs = slice(lo, hi)
        a = [jnp.dot(h, wg_ref[:, cs], preferred_element_type=F32) for h in h2]
        u = [jnp.dot(h, wu_ref[:, cs], preferred_element_type=F32) for h in h2]
        act = [(ai * (1.0 / (1.0 + jnp.exp(-ai))) * ui).astype(BF16) for ai, ui in zip(a, u)]
        ff = [f + jnp.dot(t, wd_ref[cs, :], preferred_element_type=F32) for f, t in zip(ff, act)]
    for rs, t, f in zip(groups, x1, ff):
        y_ref[rs, :] = rms(t + f, gf_ref[...])


def _out_ffn(x2, attn, o_f, o_b, og, gn, wo, g2, wg, wu, wd, gf):
    t = x2.shape[0]
    tm = TOKEN_TILE
    row = lambda i: (i, 0)
    return pl.pallas_call(
        _out_ffn_kernel,
        grid=(t // tm,),
        in_specs=[
            pl.BlockSpec((tm, D_MODEL), row),
            pl.BlockSpec((tm, ATTN_WIDTH), row),
            pl.BlockSpec((tm, GLA_VW), row),
            pl.BlockSpec((tm, GLA_VW), row),
            pl.BlockSpec((tm, GLA_VW), row),
            _const_spec((1, GLA_VW)),
            _const_spec((D_MODEL, D_MODEL)),
            _const_spec((1, D_MODEL)),
            _const_spec((D_MODEL, D_FF)),
            _const_spec((D_MODEL, D_FF)),
            _const_spec((D_FF, D_MODEL)),
            _const_spec((1, D_MODEL)),
        ],
        out_specs=pl.BlockSpec((tm, D_MODEL), row),
        out_shape=jax.ShapeDtypeStruct((t, D_MODEL), F32),
        compiler_params=pltpu.CompilerParams(
            dimension_semantics=("parallel",), vmem_limit_bytes=VMEM_LIMIT),
        name="out_ffn",
    )(x2, attn, o_f, o_b, og, gn, wo, g2, wg, wu, wd, gf)


def _rope_tables(seq):
    inv_freq = ROPE_THETA ** (-jnp.arange(0, HEAD_DIM, 2, dtype=F32) / HEAD_DIM)
    ang = jnp.arange(seq, dtype=F32)[:, None] * inv_freq[None, :]
    cos = jnp.tile(jnp.cos(ang), (1, LANES // (HEAD_DIM // 2)))
    sin = jnp.tile(jnp.sin(ang), (1, LANES // (HEAD_DIM // 2)))
    second_half = (jnp.arange(LANES) % HEAD_DIM) >= HEAD_DIM // 2
    sina = jnp.where(second_half[None, :], sin, 0.0)
    sinb = jnp.where(second_half[None, :], 0.0, -sin)
    return cos, sina, sinb


def _trunk(x, p):
    b, seq, _ = x.shape
    assert seq % SUPER_BLOCK == 0 and seq // DILATIONS[-1] >= K_WINDOW
    assert seq % GLA_BLOCK == 0 and (b * seq) % TOKEN_TILE == 0 and seq % TOKEN_TILE == 0
    x2 = x.reshape(b * seq, D_MODEL)
    cos, sina, sinb = _rope_tables(seq)
    *qkv, qg, kg, vg, og, la_f, la_b = _inproj(
        x2, seq, p["g1"], p["w_in"], cos, sina, sinb, p["gate_up"], p["gate_bias"])
    r3 = lambda a: a.reshape(b, seq, a.shape[-1])
    attn = _attention(qkv, b, seq)
    o_f, o_b = _gla(r3(qg), r3(kg), r3(vg), r3(la_f), r3(la_b))
    r2 = lambda a: a.reshape(b * seq, a.shape[-1])
    y = _out_ffn(x2, r2(attn), r2(o_f), r2(o_b), og, p["gn"], p["w_out"], p["g2"],
                 p["w_gate"], p["w_up"], p["w_down"], p["gf"])
    return y.reshape(b, seq, D_MODEL)


def kernel(x_prompt, x_sample, norm1_g, w_in, gate_up_fwd, gate_bias_fwd, gate_up_bwd, gate_bias_bwd,
           gla_norm_g, w_out, norm2_g, w_gate, w_up, w_down, final_norm_g):
    assert norm1_g.shape[0] == 1, "single layer"
    w_in_p = jnp.pad(w_in[0], ((0, 0), (0, IN_COLS_PAD - IN_COLS))).astype(BF16)
    up = jnp.zeros((LANES, 2 * GLA_KW), F32)
    up = up.at[:GLA_GATE_RANK, :GLA_KW].set(gate_up_fwd[0])
    up = up.at[GLA_GATE_RANK:2 * GLA_GATE_RANK, GLA_KW:].set(gate_up_bwd[0])
    p = {
        "g1": norm1_g[0][None, :],
        "w_in": w_in_p,
        "gate_up": up.astype(BF16),
        "gate_bias": jnp.concatenate([gate_bias_fwd[0], gate_bias_bwd[0]])[None, :],
        "gn": gla_norm_g[0][None, :],
        "w_out": w_out[0].astype(BF16),
        "g2": norm2_g[0][None, :],
        "w_gate": w_gate[0].astype(BF16),
        "w_up": w_up[0].astype(BF16),
        "w_down": w_down[0].astype(BF16),
        "gf": final_norm_g[None, :],
    }
    return _trunk(x_prompt, p), _trunk(x_sample, p)
```

```python
import functools

import jax
import jax.numpy as jnp
from jax import lax
from jax.experimental import pallas as pl
from jax.experimental.pallas import tpu as pltpu

F32 = jnp.float32
BF16 = jnp.bfloat16

D_MODEL = 1024
HEAD_DIM = 64
ATTN_HEADS = 8
ATTN_WIDTH = ATTN_HEADS * HEAD_DIM
DILATIONS = (1, 4, 16)
HALF_WINDOW = 64
ROPE_THETA = 10000.0
GLA_HEADS = 4
GLA_DK = 64
GLA_DV = 128
GLA_KW = GLA_HEADS * GLA_DK
GLA_VW = GLA_HEADS * GLA_DV
GLA_GATE_RANK = 16
GLA_TAU = 16.0
GLA_CHUNK = 64
D_FF = 2816
EPS = 1e-6
MASK_VALUE = -1e30
LOG2_E = 1.4426950408889634

LANES = 128
IN_COLS = 3 * ATTN_WIDTH + 2 * GLA_KW + 2 * GLA_VW + 2 * GLA_GATE_RANK
IN_COLS_PAD = 3200
GATE_COL0 = 3 * ATTN_WIDTH + 2 * GLA_KW + 2 * GLA_VW
VMEM_LIMIT = 56 * 1024 * 1024

TOKEN_TILE = 512
Q_BLOCK = 128
K_WINDOW = Q_BLOCK + 2 * HALF_WINDOW
SUPER_BLOCK = Q_BLOCK * DILATIONS[-1]
ATTN_UNROLL = 8
GLA_BLOCK = 1024
FF_CHUNKS = ((0, 1536), (1536, D_FF))
INPROJ_ROW_SPLIT = 2
OUT_ROW_SPLIT = 4


def _const_spec(shape):
    return pl.BlockSpec(shape, lambda *_: (0,) * len(shape), pipeline_mode=pl.Buffered(1))


def _inproj_kernel(x_ref, g_ref, w_ref, cos_ref, sina_ref, sinb_ref, up_ref, gb_ref,
                   q1_ref, q4_ref, q16_ref, k1_ref, k4_ref, k16_ref, v1_ref, v4_ref, v16_ref,
                   qg_ref, kg_ref, vg_ref, og_ref, laf_ref, lab_ref,
                   q_stage, k_stage, v_stage, q_stage4, k_stage4, v_stage4):
    rows_per_group = TOKEN_TILE // INPROJ_ROW_SPLIT
    groups = [slice(i * rows_per_group, (i + 1) * rows_per_group)
              for i in range(INPROJ_ROW_SPLIT)]
    hns = []
    for rs in groups:
        x = x_ref[rs, :]
        ms = jnp.mean(x * x, axis=-1, keepdims=True)
        hns.append((x * lax.rsqrt(ms + EPS) * g_ref[...]).astype(BF16))

    def proj(hn, col0, width):
        return jnp.dot(hn, w_ref[:, col0:col0 + width], preferred_element_type=F32)

    def rope(t, rs):
        cos = cos_ref[rs, :]
        sina = sina_ref[rs, :]
        sinb = sinb_ref[rs, :]
        out = []
        for j in range(t.shape[1] // LANES):
            s = t[:, j * LANES:(j + 1) * LANES]
            out.append(s * cos + pltpu.roll(s, HEAD_DIM // 2, 1) * sina
                       + pltpu.roll(s, LANES - HEAD_DIM // 2, 1) * sinb)
        return jnp.concatenate(out, axis=1)

    def emit(t, rs, stage, stage4, outs):
        out1, out4, out16 = outs
        n = rs.stop - rs.start
        q4 = slice(rs.start // 4, rs.stop // 4)
        q16 = slice(rs.start // 16, rs.stop // 16)
        for j in range(ATTN_WIDTH // LANES):
            slab = t[:, j * LANES:(j + 1) * LANES]
            stage[j, rs, :] = slab
            out1[j, rs, :] = slab.astype(BF16)
            for r in range(4):
                piece = stage[j, pl.ds(rs.start + r, n // 4, stride=4), :]
                out4[j, r, q4, :] = piece.astype(BF16)
                stage4[j, r, q4, :] = piece
            for r16 in range(16):
                piece = stage4[j, r16 % 4, pl.ds(q4.start + r16 // 4, n // 16, stride=4), :]
                out16[j, r16, q16, :] = piece.astype(BF16)

    c = 3 * ATTN_WIDTH
    for rs, hn in zip(groups, hns):
        r = proj(hn, GATE_COL0, LANES).astype(BF16)
        z = jnp.dot(r, up_ref[...], preferred_element_type=F32) + gb_ref[...]
        la = (jnp.minimum(z, 0.0) - jnp.log(1.0 + jnp.exp(-jnp.abs(z)))) * (1.0 / GLA_TAU)
        laf_ref[rs, :] = la[:, :GLA_KW]
        lab_ref[rs, :] = la[:, GLA_KW:]
    for rs, hn in zip(groups, hns):
        emit(rope(proj(hn, 0, ATTN_WIDTH), rs) * (HEAD_DIM ** -0.5 * LOG2_E), rs,
             q_stage, q_stage4, (q1_ref, q4_ref, q16_ref))
    for rs, hn in zip(groups, hns):
        emit(rope(proj(hn, ATTN_WIDTH, ATTN_WIDTH), rs), rs, k_stage, k_stage4,
             (k1_ref, k4_ref, k16_ref))
    for rs, hn in zip(groups, hns):
        emit(proj(hn, 2 * ATTN_WIDTH, ATTN_WIDTH), rs, v_stage, v_stage4,
             (v1_ref, v4_ref, v16_ref))
    for rs, hn in zip(groups, hns):
        qg_ref[rs, :] = proj(hn, c, GLA_KW) * (GLA_DK ** -0.5)
    for rs, hn in zip(groups, hns):
        kg_ref[rs, :] = proj(hn, c + GLA_KW, GLA_KW)
    for rs, hn in zip(groups, hns):
        vg_ref[rs, :] = proj(hn, c + 2 * GLA_KW, GLA_VW).astype(BF16)
    for rs, hn in zip(groups, hns):
        og_ref[rs, :] = proj(hn, c + 2 * GLA_KW + GLA_VW, GLA_VW).astype(BF16)


def _inproj(x2, seq, g, w, cos, sina, sinb, up, gb):
    t = x2.shape[0]
    b = t // seq
    tm = TOKEN_TILE
    n_pos = seq // tm
    n_pair = ATTN_WIDTH // LANES
    row = lambda i: (i, 0)
    pos = lambda i: (i % n_pos, 0)
    attn_shapes, attn_specs = [], []
    for _ in range(3):
        for d in DILATIONS:
            if d == 1:
                attn_shapes.append(jax.ShapeDtypeStruct((n_pair, t, LANES), BF16))
                attn_specs.append(pl.BlockSpec((n_pair, tm, LANES), lambda i: (0, i, 0)))
            else:
                attn_shapes.append(jax.ShapeDtypeStruct((n_pair, b, d, seq // d, LANES), BF16))
                attn_specs.append(pl.BlockSpec((n_pair, None, d, tm // d, LANES),
                                               lambda i: (0, i // n_pos, 0, i % n_pos, 0)))
    out_shapes = attn_shapes + [
        jax.ShapeDtypeStruct((t, GLA_KW), F32),
        jax.ShapeDtypeStruct((t, GLA_KW), F32),
        jax.ShapeDtypeStruct((t, GLA_VW), BF16),
        jax.ShapeDtypeStruct((t, GLA_VW), BF16),
        jax.ShapeDtypeStruct((t, GLA_KW), F32),
        jax.ShapeDtypeStruct((t, GLA_KW), F32),
    ]
    return pl.pallas_call(
        _inproj_kernel,
        grid=(t // tm,),
        in_specs=[
            pl.BlockSpec((tm, D_MODEL), row),
            _const_spec((1, D_MODEL)),
            _const_spec((D_MODEL, IN_COLS_PAD)),
            pl.BlockSpec((tm, LANES), pos),
            pl.BlockSpec((tm, LANES), pos),
            pl.BlockSpec((tm, LANES), pos),
            _const_spec((LANES, 2 * GLA_KW)),
            _const_spec((1, 2 * GLA_KW)),
        ],
        out_specs=attn_specs + [pl.BlockSpec((tm, s.shape[1]), row)
                                for s in out_shapes[len(attn_specs):]],
        out_shape=out_shapes,
        scratch_shapes=([pltpu.VMEM((n_pair, tm, LANES), F32)] * 3
                        + [pltpu.VMEM((n_pair, 4, tm // 4, LANES), F32)] * 3),
        compiler_params=pltpu.CompilerParams(
            dimension_semantics=("parallel",), vmem_limit_bytes=VMEM_LIMIT),
        name="inproj",
    )(x2, g, w, cos, sina, sinb, up, gb)


def _attn_kernel(q1_ref, q4_ref, q16_ref, k1_ref, k4_ref, k16_ref, v1_ref, v4_ref, v16_ref,
                 o_ref, o4_ref, lse4_ref, o16_ref, lse16_ref, bias_ref,
                 s_stage, p_stage, m_stage, *, seq):
    lane = lax.broadcasted_iota(jnp.int32, (Q_BLOCK, LANES), 1)
    head0 = lane < HEAD_DIM
    ones = jnp.ones((K_WINDOW, LANES), BF16)
    qkv = {1: (q1_ref, k1_ref, v1_ref), 4: (q4_ref, k4_ref, v4_ref),
           16: (q16_ref, k16_ref, v16_ref)}

    col_minus_row = (lax.broadcasted_iota(jnp.int32, (Q_BLOCK, K_WINDOW), 1)
                     - lax.broadcasted_iota(jnp.int32, (Q_BLOCK, K_WINDOW), 0))
    for e in range(bias_ref.shape[0]):
        bias_ref[e] = jnp.where(jnp.abs(col_minus_row - HALF_WINDOW * e) <= HALF_WINDOW,
                                0.0, MASK_VALUE)

    def window(d, cls, sub0):
        n_sub = seq // d
        win0 = jnp.clip(sub0 - HALF_WINDOW, 0, n_sub - K_WINDOW)
        q_rows = pl.ds(pl.multiple_of(sub0, Q_BLOCK), Q_BLOCK)
        k_rows = pl.ds(pl.multiple_of(win0, HALF_WINDOW), K_WINDOW)
        q_ref, k_ref, v_ref = qkv[d]
        if d > 1:
            load = lambda ref, rows: ref[cls, rows, :]
        else:
            load = lambda ref, rows: ref[rows, :]
        return (lambda: load(q_ref, q_rows), lambda: load(k_ref, k_rows),
                lambda: load(v_ref, k_rows), (sub0 - win0) // HALF_WINDOW)

    def patterns(units):
        wins = [window(*u) for u in units]
        for u, (load_q, load_k, _, case) in enumerate(wins):
            q = load_q()
            k = load_k()
            zero = jnp.zeros_like(q)
            q2 = jnp.concatenate([jnp.where(head0, q, zero), jnp.where(head0, zero, q)], axis=0)
            s = lax.dot_general(q2, k, (((1,), (1,)), ((), ())), preferred_element_type=F32)
            bias = bias_ref[case]
            s_stage[u] = s + jnp.concatenate([bias, bias], axis=0)
        for u in range(len(units)):
            ms = []
            for h in range(2):
                sh = s_stage[u, h * Q_BLOCK:(h + 1) * Q_BLOCK, :]
                m = jnp.max(sh, axis=-1, keepdims=True)
                ms.append(jnp.broadcast_to(m, (Q_BLOCK, LANES)))
                p_stage[u, h * Q_BLOCK:(h + 1) * Q_BLOCK, :] = jnp.exp2(sh - m).astype(BF16)
            m_stage[u] = jnp.where(head0, ms[0], ms[1])
        res = []
        for u, (_, _, load_v, _) in enumerate(wins):
            v = jnp.concatenate([load_v(), ones], axis=1)
            pv = jnp.dot(p_stage[u], v, preferred_element_type=F32)
            l = jnp.where(head0, pv[:Q_BLOCK, LANES:], pv[Q_BLOCK:, LANES:])
            out = jnp.where(head0, pv[:Q_BLOCK, :LANES], pv[Q_BLOCK:, :LANES]) / l
            res.append((out, m_stage[u] + jnp.log2(l)))
        return res

    n_iter = SUPER_BLOCK // Q_BLOCK // ATTN_UNROLL

    def super_block(sb, carry):
        base = pl.multiple_of(sb * SUPER_BLOCK, SUPER_BLOCK)

        for d, o_d, lse_d in ((16, o16_ref, lse16_ref), (4, o4_ref, lse4_ref)):
            blocks_per_class = SUPER_BLOCK // d // Q_BLOCK

            def park(it, c, d=d, bpc=blocks_per_class, o_d=o_d, lse_d=lse_d):
                idx = [it * ATTN_UNROLL + u for u in range(ATTN_UNROLL)]
                cls = [i // bpc for i in idx]
                blk = [(i % bpc) * Q_BLOCK for i in idx]
                res = patterns([(d, cl, sb * (SUPER_BLOCK // d) + bl)
                                for cl, bl in zip(cls, blk)])
                for (out, lse), cl, bl in zip(res, cls, blk):
                    rows = pl.ds(cl + d * bl, Q_BLOCK, stride=d)
                    o_d[rows, :] = out
                    lse_d[rows, :] = lse
                return c

            lax.fori_loop(0, n_iter, park, 0)

        def merge(it, c):
            blk = [pl.multiple_of((it * ATTN_UNROLL + u) * Q_BLOCK, Q_BLOCK)
                   for u in range(ATTN_UNROLL)]
            res = patterns([(1, 0, base + bl) for bl in blk])
            for (out1, lse1), bl in zip(res, blk):
                rows = pl.ds(bl, Q_BLOCK)
                lse4 = lse4_ref[rows, :]
                lse16 = lse16_ref[rows, :]
                top = jnp.maximum(jnp.maximum(lse1, lse4), lse16)
                w1 = jnp.exp2(lse1 - top)
                w4 = jnp.exp2(lse4 - top)
                w16 = jnp.exp2(lse16 - top)
                mix = ((w1 * out1 + w4 * o4_ref[rows, :] + w16 * o16_ref[rows, :])
                       / (w1 + w4 + w16))
                o_ref[pl.ds(base + bl, Q_BLOCK), :] = mix.astype(o_ref.dtype)
            return c

        lax.fori_loop(0, n_iter, merge, 0)
        return carry

    lax.fori_loop(0, seq // SUPER_BLOCK, super_block, 0)


def _attention(qkv, b, seq):
    in_specs = []
    for a in qkv:
        if a.ndim == 3:
            in_specs.append(pl.BlockSpec((None, seq, LANES), lambda i, h: (h, i, 0)))
        else:
            in_specs.append(pl.BlockSpec((None, None) + a.shape[2:],
                                         lambda i, h: (h, i, 0, 0, 0)))
    stat = pltpu.VMEM((SUPER_BLOCK, LANES), F32)
    return pl.pallas_call(
        functools.partial(_attn_kernel, seq=seq),
        grid=(b, ATTN_WIDTH // LANES),
        in_specs=in_specs,
        out_specs=pl.BlockSpec((None, seq, LANES), lambda i, h: (i, 0, h)),
        out_shape=jax.ShapeDtypeStruct((b, seq, ATTN_WIDTH), BF16),
        scratch_shapes=[stat, stat, stat, stat, pltpu.VMEM((3, Q_BLOCK, K_WINDOW), F32),
                        pltpu.VMEM((ATTN_UNROLL, 2 * Q_BLOCK, K_WINDOW), F32),
                        pltpu.VMEM((ATTN_UNROLL, 2 * Q_BLOCK, K_WINDOW), BF16),
                        pltpu.VMEM((ATTN_UNROLL, Q_BLOCK, LANES), F32)],
        compiler_params=pltpu.CompilerParams(
            dimension_semantics=("parallel", "parallel"), vmem_limit_bytes=VMEM_LIMIT),
        name="dilated_attn",
    )(*qkv)


def _gla_kernel(qf_ref, kf_ref, vf_ref, laf_ref, qb_ref, kb_ref, vb_ref, lab_ref,
                of_ref, ob_ref, stf_ref, stb_ref):
    c = GLA_CHUNK

    @pl.when(pl.program_id(1) == 0)
    def _():
        stf_ref[...] = jnp.zeros_like(stf_ref)
        stb_ref[...] = jnp.zeros_like(stb_ref)

    row = lax.broadcasted_iota(jnp.int32, (c, 2 * c), 0)
    col = lax.broadcasted_iota(jnp.int32, (c, 2 * c), 1) % c
    tri_f = jnp.where(col <= row, 1.0, 0.0).astype(BF16)
    tri_b = jnp.where(col >= row, 1.0, 0.0).astype(BF16)
    row4 = lax.broadcasted_iota(jnp.int32, (c, GLA_HEADS * c), 0)
    col4 = lax.broadcasted_iota(jnp.int32, (c, GLA_HEADS * c), 1) % c
    keep_f = col4 <= row4
    keep_b = col4 > row4
    klane = lax.broadcasted_iota(jnp.int32, (c, GLA_KW), 1) // GLA_DK
    zeros_v = jnp.zeros((c, GLA_DV), BF16)
    nt = (((1,), (1,)), ((), ()))
    tn = (((0,), (0,)), ((), ()))

    def by_head(t, lane_head):
        return jnp.concatenate(
            [jnp.where(lane_head == h, t, 0.0) for h in range(GLA_HEADS)], axis=0).astype(BF16)

    n_chunks = GLA_BLOCK // c
    streams = (
        (qf_ref, kf_ref, vf_ref, laf_ref, of_ref, stf_ref, tri_f, keep_f, c - 1,
         [j * c for j in range(n_chunks)]),
        (qb_ref, kb_ref, vb_ref, lab_ref, ob_ref, stb_ref, tri_b, keep_b, 0,
         [(n_chunks - 1 - j) * c for j in range(n_chunks)]),
    )
    jobs = [(s, streams[s][9][j]) for j in range(n_chunks) for s in range(2)]

    def rows(ref, r0):
        return ref[r0:r0 + c, :]

    cums = []
    for s, r0 in jobs:
        la = rows(streams[s][3], r0)
        la_hi = la.astype(BF16)
        la_lo = (la - la_hi.astype(F32)).astype(BF16)
        cums.append(jnp.dot(streams[s][6], jnp.concatenate([la_hi, la_lo], axis=0),
                            preferred_element_type=F32))

    q_es, decays, attns, kv_ts = [], [], [], []
    for (s, r0), cum in zip(jobs, cums):
        q_ref, k_ref, v_ref = streams[s][0:3]
        last_row = streams[s][8]
        cum_last = cum[last_row:last_row + 1, :]
        k = rows(k_ref, r0)
        q_e_f32 = rows(q_ref, r0) * jnp.exp(cum)
        q_e = q_e_f32.astype(BF16)
        k_e = k * jnp.exp(-cum)
        k_end = k * jnp.exp(cum_last - cum)
        q_es.append(by_head(q_e_f32, klane))
        decays.append(jnp.exp(cum_last))
        attns.append(lax.dot_general(q_e, by_head(k_e, klane), nt, preferred_element_type=F32))
        v = rows(v_ref, r0)
        v_heads = jnp.concatenate(
            [v[:, h * GLA_DV:(h + 1) * GLA_DV] for h in range(GLA_HEADS)], axis=0)
        kv_ts.append(lax.dot_general(v_heads, by_head(k_end, klane), tn,
                                     preferred_element_type=F32))

    o_intras = []
    for (s, r0), attn in zip(jobs, attns):
        v = rows(streams[s][2], r0)
        v_h = [v[:, h * GLA_DV:(h + 1) * GLA_DV] for h in range(GLA_HEADS)]
        v_diag = jnp.concatenate(
            [jnp.concatenate([v_h[h] if h == g else zeros_v for h in range(GLA_HEADS)], axis=0)
             for g in range(GLA_HEADS)], axis=1)
        attn = jnp.where(streams[s][7], attn, 0.0).astype(BF16)
        o_intras.append(jnp.dot(attn, v_diag, preferred_element_type=F32))

    st = [streams[0][5][...], streams[1][5][...]]
    for (s, r0), q_heads, decay, kv_t, o_intra in zip(jobs, q_es, decays, kv_ts, o_intras):
        o_state = lax.dot_general(q_heads, st[s].astype(BF16), nt, preferred_element_type=F32)
        streams[s][4][r0:r0 + c, :] = o_intra + jnp.concatenate(
            [o_state[h * c:(h + 1) * c] for h in range(GLA_HEADS)], axis=1)
        st[s] = st[s] * decay + kv_t
    streams[0][5][...] = st[0]
    streams[1][5][...] = st[1]


def _gla(q, k, v, la_f, la_b):
    b, seq, _ = q.shape
    n = seq // GLA_BLOCK
    fwd = lambda w: pl.BlockSpec((None, GLA_BLOCK, w), lambda i, j: (i, j, 0))
    bwd = lambda w: pl.BlockSpec((None, GLA_BLOCK, w), lambda i, j: (i, n - 1 - j, 0))
    out = jax.ShapeDtypeStruct((b, seq, GLA_VW), F32)
    state = pltpu.VMEM((GLA_DV, GLA_KW), F32)
    return pl.pallas_call(
        _gla_kernel,
        grid=(b, n),
        in_specs=[fwd(GLA_KW), fwd(GLA_KW), fwd(GLA_VW), fwd(GLA_KW),
                  bwd(GLA_KW), bwd(GLA_KW), bwd(GLA_VW), bwd(GLA_KW)],
        out_specs=[fwd(GLA_VW), bwd(GLA_VW)],
        out_shape=[out, out],
        scratch_shapes=[state, state],
        compiler_params=pltpu.CompilerParams(
            dimension_semantics=("parallel", "arbitrary"), vmem_limit_bytes=VMEM_LIMIT),
        name="gla",
    )(q, k, v, la_f, q, k, v, la_b)


def _out_ffn_kernel(x_ref, attn_ref, of_ref, ob_ref, og_ref, gn_ref, wo_ref, g2_ref,
                    wg_ref, wu_ref, wd_ref, gf_ref, y_ref):
    def rms(t, g):
        return t * lax.rsqrt(jnp.mean(t * t, axis=-1, keepdims=True) + EPS) * g

    rows_per_group = TOKEN_TILE // OUT_ROW_SPLIT
    groups = [slice(i * rows_per_group, (i + 1) * rows_per_group) for i in range(OUT_ROW_SPLIT)]
    mixed = []
    for rs in groups:
        o = of_ref[rs, :] + ob_ref[rs, :]
        og = og_ref[rs, :].astype(F32)
        gate = og * (1.0 / (1.0 + jnp.exp(-og)))
        heads = []
        for h in range(GLA_HEADS):
            sl = slice(h * GLA_DV, (h + 1) * GLA_DV)
            heads.append(rms(o[:, sl], gn_ref[:, sl]) * gate[:, sl])
        mixed.append(jnp.concatenate([attn_ref[rs, :]] + [t.astype(BF16) for t in heads], axis=1))
    x1 = [x_ref[rs, :] + jnp.dot(m, wo_ref[...], preferred_element_type=F32)
          for rs, m in zip(groups, mixed)]
    h2 = [rms(t, g2_ref[...]).astype(BF16) for t in x1]
    ff = [jnp.zeros_like(t) for t in x1]
    for lo, hi in FF_CHUNKS:
        cs = slice(lo, hi)
        a = [jnp.dot(h, wg_ref[:, cs], preferred_element_type=F32) for h in h2]
        u = [jnp.dot(h, wu_ref[:, cs], preferred_element_type=F32) for h in h2]
        act = [(ai * (1.0 / (1.0 + jnp.exp(-ai))) * ui).astype(BF16) for ai, ui in zip(a, u)]
        ff = [f + jnp.dot(t, wd_ref[cs, :], preferred_element_type=F32) for f, t in zip(ff, act)]
    for rs, t, f in zip(groups, x1, ff):
        y_ref[rs, :] = rms(t + f, gf_ref[...])


def _out_ffn(x2, attn, o_f, o_b, og, gn, wo, g2, wg, wu, wd, gf):
    t = x2.shape[0]
    tm = TOKEN_TILE
    row = lambda i: (i, 0)
    return pl.pallas_call(
        _out_ffn_kernel,
        grid=(t // tm,),
        in_specs=[
            pl.BlockSpec((tm, D_MODEL), row),
            pl.BlockSpec((tm, ATTN_WIDTH), row),
            pl.BlockSpec((tm, GLA_VW), row),
            pl.BlockSpec((tm, GLA_VW), row),
            pl.BlockSpec((tm, GLA_VW), row),
            _const_spec((1, GLA_VW)),
            _const_spec((D_MODEL, D_MODEL)),
            _const_spec((1, D_MODEL)),
            _const_spec((D_MODEL, D_FF)),
            _const_spec((D_MODEL, D_FF)),
            _const_spec((D_FF, D_MODEL)),
            _const_spec((1, D_MODEL)),
        ],
        out_specs=pl.BlockSpec((tm, D_MODEL), row),
        out_shape=jax.ShapeDtypeStruct((t, D_MODEL), F32),
        compiler_params=pltpu.CompilerParams(
            dimension_semantics=("parallel",), vmem_limit_bytes=VMEM_LIMIT),
        name="out_ffn",
    )(x2, attn, o_f, o_b, og, gn, wo, g2, wg, wu, wd, gf)


def _rope_tables(seq):
    inv_freq = ROPE_THETA ** (-jnp.arange(0, HEAD_DIM, 2, dtype=F32) / HEAD_DIM)
    ang = jnp.arange(seq, dtype=F32)[:, None] * inv_freq[None, :]
    cos = jnp.tile(jnp.cos(ang), (1, LANES // (HEAD_DIM // 2)))
    sin = jnp.tile(jnp.sin(ang), (1, LANES // (HEAD_DIM // 2)))
    second_half = (jnp.arange(LANES) % HEAD_DIM) >= HEAD_DIM // 2
    sina = jnp.where(second_half[None, :], sin, 0.0)
    sinb = jnp.where(second_half[None, :], 0.0, -sin)
    return cos, sina, sinb


def _trunk(x, p):
    b, seq, _ = x.shape
    assert seq % SUPER_BLOCK == 0 and seq // DILATIONS[-1] >= K_WINDOW
    assert seq % GLA_BLOCK == 0 and (b * seq) % TOKEN_TILE == 0 and seq % TOKEN_TILE == 0
    x2 = x.reshape(b * seq, D_MODEL)
    cos, sina, sinb = _rope_tables(seq)
    *qkv, qg, kg, vg, og, la_f, la_b = _inproj(
        x2, seq, p["g1"], p["w_in"], cos, sina, sinb, p["gate_up"], p["gate_bias"])
    r3 = lambda a: a.reshape(b, seq, a.shape[-1])
    attn = _attention(qkv, b, seq)
    o_f, o_b = _gla(r3(qg), r3(kg), r3(vg), r3(la_f), r3(la_b))
    r2 = lambda a: a.reshape(b * seq, a.shape[-1])
    y = _out_ffn(x2, r2(attn), r2(o_f), r2(o_b), og, p["gn"], p["w_out"], p["g2"],
                 p["w_gate"], p["w_up"], p["w_down"], p["gf"])
    return y.reshape(b, seq, D_MODEL)


def kernel(x_prompt, x_sample, norm1_g, w_in, gate_up_fwd, gate_bias_fwd, gate_up_bwd, gate_bias_bwd,
           gla_norm_g, w_out, norm2_g, w_gate, w_up, w_down, final_norm_g):
    assert norm1_g.shape[0] == 1, "single layer"
    w_in_p = jnp.pad(w_in[0], ((0, 0), (0, IN_COLS_PAD - IN_COLS))).astype(BF16)
    up = jnp.zeros((LANES, 2 * GLA_KW), F32)
    up = up.at[:GLA_GATE_RANK, :GLA_KW].set(gate_up_fwd[0])
    up = up.at[GLA_GATE_RANK:2 * GLA_GATE_RANK, GLA_KW:].set(gate_up_bwd[0])
    p = {
        "g1": norm1_g[0][None, :],
        "w_in": w_in_p,
        "gate_up": up.astype(BF16),
        "gate_bias": jnp.concatenate([gate_bias_fwd[0], gate_bias_bwd[0]])[None, :],
        "gn": gla_norm_g[0][None, :],
        "w_out": w_out[0].astype(BF16),
        "g2": norm2_g[0][None, :],
        "w_gate": w_gate[0].astype(BF16),
        "w_up": w_up[0].astype(BF16),
        "w_down": w_down[0].astype(BF16),
        "gf": final_norm_g[None, :],
    }
    return _trunk(x_prompt, p), _trunk(x_sample, p)
```
